```python
import math
import jax, jax.numpy as jnp
from jax import lax
import numpy as np

D_MODEL = 1024
BATCH = 4
SEQ = 8192
DEPTH = 2
DEC_BATCH = 8
DEC_SEQ = 2048
PAST_LEN = 128

D_MIX = D_MODEL
MLA_HEADS = 8
MLA_NOPE = 64
MLA_ROPE = 32
MLA_V = 64
Q_LORA = 256
KV_LORA = 128
ROPE_THETA = 10000.0
Q_BLOCK = 128
DIL_HEADS = 8
DIL_HEAD_DIM = 64
DIL_PAIRS = ((128, 1), (512, 4), (2048, 16))
D_FF = 4 * D_MODEL
NORM_EPS = 1e-6
NEG_BIG = -1e30

MLA_OUT = MLA_HEADS * MLA_V
DIL_OUT = DIL_HEADS * DIL_HEAD_DIM
DIL_QKV = DIL_HEADS * DIL_HEAD_DIM
IN_COLS = Q_LORA + KV_LORA + MLA_ROPE + 3 * DIL_QKV

kernel_name = "hybrid_mla_dilated_encoder"


def rms_norm(x, g):
    xf = x.astype(jnp.float32)
    y = xf * lax.rsqrt(jnp.mean(xf * xf, axis=-1, keepdims=True) + NORM_EPS)
    return (y * g.astype(jnp.float32)).astype(x.dtype)


def rope_tables(S, dtype):
    half = MLA_ROPE // 2
    inv_freq = ROPE_THETA ** (-jnp.arange(half, dtype=jnp.float32) / half)
    ang = jnp.arange(S, dtype=jnp.float32)[:, None] * inv_freq[None, :]
    return jnp.cos(ang).astype(dtype), jnp.sin(ang).astype(dtype)


def apply_rope(x, cos, sin):
    half = x.shape[-1] // 2
    x1, x2 = x[..., :half], x[..., half:]
    c = cos[None, :, None, :]
    s = sin[None, :, None, :]
    return jnp.concatenate([x1 * c - x2 * s, x1 * s + x2 * c], axis=-1)


def mla_attention(lat_q, lat_kv, k_rope_raw, g_q_lat, w_uq, g_kv_lat, w_ukv):
    B, S, _ = lat_q.shape
    cos, sin = rope_tables(S, lat_q.dtype)
    q = jnp.einsum('bsr,rhd->bshd', rms_norm(lat_q, g_q_lat), w_uq)
    q = jnp.concatenate([q[..., :MLA_NOPE], apply_rope(q[..., MLA_NOPE:], cos, sin)], axis=-1)
    kv = jnp.einsum('bsr,rhd->bshd', rms_norm(lat_kv, g_kv_lat), w_ukv)
    k_nope, v = kv[..., :MLA_NOPE], kv[..., MLA_NOPE:]
    k_rope = apply_rope(k_rope_raw[:, :, None, :], cos, sin)
    k = jnp.concatenate([k_nope, jnp.broadcast_to(k_rope, (B, S, MLA_HEADS, MLA_ROPE))], axis=-1)
    scale = 1.0 / math.sqrt(MLA_NOPE + MLA_ROPE)
    nq = S // Q_BLOCK
    qb = jnp.moveaxis(q.reshape(B, nq, Q_BLOCK, MLA_HEADS, MLA_NOPE + MLA_ROPE), 1, 0)

    def attend(qblk):
        s = jnp.einsum('bqhd,bkhd->bhqk', qblk, k).astype(jnp.float32) * scale
        p = jax.nn.softmax(s, axis=-1)
        return jnp.einsum('bhqk,bkhd->bqhd', p.astype(v.dtype), v)

    o = lax.map(attend, qb)
    return jnp.moveaxis(o, 0, 1).reshape(B, S, MLA_OUT)


def fold(t, d):
    B, S = t.shape[:2]
    rest = t.shape[2:]
    t = t.reshape((B, S // d, d) + rest)
    t = jnp.moveaxis(t, 2, 1)
    return t.reshape((B * d, S // d) + rest)


def unfold(t, B, d):
    L = t.shape[1]
    rest = t.shape[2:]
    t = t.reshape((B, d, L) + rest)
    t = jnp.moveaxis(t, 1, 2)
    return t.reshape((B, L * d) + rest)


def banded_attention(q, k, v, n, d, slopes):
    Bp, L, H, D = q.shape
    nb = -(-L // n)
    Lp = nb * n
    qp = jnp.pad(q, ((0, 0), (0, Lp - L), (0, 0), (0, 0))).reshape(Bp, nb, n, H, D)
    pad_k = ((0, 0), (n, Lp - L + n), (0, 0), (0, 0))
    kr = jnp.pad(k, pad_k).reshape(Bp, nb + 2, n, H, D)
    vr = jnp.pad(v, pad_k).reshape(Bp, nb + 2, n, H, D)
    kw = jnp.concatenate([kr[:, :-2], kr[:, 1:-1], kr[:, 2:]], axis=2)
    vw = jnp.concatenate([vr[:, :-2], vr[:, 1:-1], vr[:, 2:]], axis=2)
    blk = jnp.arange(nb)[:, None]
    qidx = blk * n + jnp.arange(n)[None, :]
    kidx = blk * n - n + jnp.arange(3 * n)[None, :]
    rel = jnp.abs(kidx[:, None, :] - qidx[:, :, None])
    valid = (rel <= n) & (kidx[:, None, :] >= 0) & (kidx[:, None, :] < L)
    bias = -slopes[:, None, None, None] * (d * rel).astype(jnp.float32)[None]
    bias = jnp.moveaxis(bias, 0, 1)
    s = jnp.einsum('bnqhd,bnkhd->bnhqk', qp, kw).astype(jnp.float32) / math.sqrt(D)
    s = jnp.where(valid[:, None], s + bias[None], NEG_BIG)
    m = jnp.max(s, axis=-1)
    e = jnp.exp(s - m[..., None])
    den = jnp.sum(e, axis=-1)
    o = jnp.einsum('bnhqk,bnkhd->bnqhd', e.astype(v.dtype), vw).astype(jnp.float32)
    den_t = jnp.swapaxes(den, 2, 3)
    o = (o / den_t[..., None]).reshape(Bp, Lp, H, D)[:, :L]
    m_t = jnp.swapaxes(m, 2, 3).reshape(Bp, Lp, H)[:, :L]
    den_t = den_t.reshape(Bp, Lp, H)[:, :L]
    return o, m_t, den_t


def dilated_attention(q, k, v):
    B, S, H, D = q.shape
    slopes = 2.0 ** (-8.0 * (jnp.arange(H, dtype=jnp.float32) + 1.0) / H)
    outs, maxs, dens = [], [], []
    for window, d in DIL_PAIRS:
        n = window // (2 * d)
        o, m, den = banded_attention(fold(q, d), fold(k, d), fold(v, d), n, d, slopes)
        outs.append(unfold(o, B, d))
        maxs.append(unfold(m, B, d))
        dens.append(unfold(den, B, d))
    mx = jnp.stack(maxs)
    w = jnp.stack(dens) * jnp.exp(mx - jnp.max(mx, axis=0, keepdims=True))
    out = jnp.sum(w[..., None] * jnp.stack(outs), axis=0) / jnp.sum(w, axis=0)[..., None]
    return out.astype(q.dtype).reshape(B, S, H * D)


def encoder_layer(x, g_pre_mix, w_in, g_q_lat, w_uq, g_kv_lat, w_ukv, g_out_mla, g_out_dil,
                  w_out, g_post_mix, g_pre_mlp, w_up, w_down, g_post_mlp):
    B, S, _ = x.shape
    z = rms_norm(x, g_pre_mix) @ w_in
    c0 = Q_LORA
    c1 = c0 + KV_LORA
    c2 = c1 + MLA_ROPE
    c3 = c2 + DIL_QKV
    c4 = c3 + DIL_QKV
    a = mla_attention(z[..., :c0], z[..., c0:c1], z[..., c1:c2], g_q_lat, w_uq, g_kv_lat, w_ukv)
    hd = (B, S, DIL_HEADS, DIL_HEAD_DIM)
    b = dilated_attention(z[..., c2:c3].reshape(hd), z[..., c3:c4].reshape(hd), z[..., c4:].reshape(hd))
    mix = jnp.concatenate([rms_norm(a, g_out_mla), rms_norm(b, g_out_dil)], axis=-1) @ w_out
    x = x + rms_norm(mix, g_post_mix)
    u = jnp.square(jax.nn.relu(rms_norm(x, g_pre_mlp) @ w_up))
    return x + rms_norm(u @ w_down, g_post_mlp)


def setup_inputs(seed: int = 0) -> dict:
    key = jax.random.key(seed)
    ks = jax.random.split(key, 20)
    f32 = jnp.float32

    def nrm(k, shape, fan_in):
        return jax.random.normal(k, shape, f32) * (fan_in ** -0.5)

    def gain(k, dim):
        return 1.0 + 0.05 * jax.random.normal(k, (DEPTH, dim), f32)

    return {
        "x_prompt": jax.random.normal(ks[0], (BATCH, SEQ, D_MODEL), f32),
        "x_sample": jax.random.normal(ks[1], (DEC_BATCH, DEC_SEQ, D_MODEL), f32),
        "g_pre_mix": gain(ks[2], D_MODEL),
        "w_in": nrm(ks[3], (DEPTH, D_MODEL, IN_COLS), D_MODEL),
        "g_q_lat": gain(ks[4], Q_LORA),
        "w_uq": nrm(ks[5], (DEPTH, Q_LORA, MLA_HEADS, MLA_NOPE + MLA_ROPE), Q_LORA),
        "g_kv_lat": gain(ks[6], KV_LORA),
        "w_ukv": nrm(ks[7], (DEPTH, KV_LORA, MLA_HEADS, MLA_NOPE + MLA_V), KV_LORA),
        "g_out_mla": gain(ks[8], MLA_OUT),
        "g_out_dil": gain(ks[9], DIL_OUT),
        "w_out": nrm(ks[10], (DEPTH, D_MIX, D_MODEL), D_MIX),
        "g_post_mix": gain(ks[11], D_MODEL),
        "g_pre_mlp": gain(ks[12], D_MODEL),
        "w_up": nrm(ks[13], (DEPTH, D_MODEL, D_FF), D_MODEL),
        "w_down": nrm(ks[14], (DEPTH, D_FF, D_MODEL), D_FF),
        "g_post_mlp": gain(ks[15], D_MODEL),
    }


def reference(x_prompt, x_sample, g_pre_mix, w_in, g_q_lat, w_uq, g_kv_lat, w_ukv, g_out_mla,
              g_out_dil, w_out, g_post_mix, g_pre_mlp, w_up, w_down, g_post_mlp):
    def trunk(x):
        for l in range(DEPTH):
            x = encoder_layer(x, g_pre_mix[l], w_in[l], g_q_lat[l], w_uq[l], g_kv_lat[l], w_ukv[l],
                              g_out_mla[l], g_out_dil[l], w_out[l], g_post_mix[l], g_pre_mlp[l],
                              w_up[l], w_down[l], g_post_mlp[l])
        return x

    y_prompt = trunk(x_prompt)
    y_sample = trunk(x_sample)
    return (y_prompt, y_sample)
```

```python
import functools
import math

import jax
import jax.numpy as jnp
from jax import lax
from jax.experimental import pallas as pl
from jax.experimental.pallas import tpu as pltpu

F32 = jnp.float32
BF16 = jnp.bfloat16

D_MODEL = 1024
MLA_HEADS = 8
MLA_NOPE = 64
MLA_ROPE = 32
MLA_V = 64
Q_LORA = 256
KV_LORA = 128
ROPE_THETA = 10000.0
DIL_HEADS = 8
DIL_HEAD_DIM = 64
DIL_PAIRS = ((128, 1), (512, 4), (2048, 16))
D_FF = 4 * D_MODEL
NORM_EPS = 1e-6

LANES = 128
HEAD_PAD = LANES
MLA_QK = MLA_HEADS * HEAD_PAD
MLA_OUT = MLA_HEADS * MLA_V
DIL_W = DIL_HEADS * DIL_HEAD_DIM
Z_COLS = 2048
Z_KROPE = Q_LORA + KV_LORA
Z_DIL = 512
LOG2E = 1.4426950408889634
MLA_QSCALE = LOG2E / math.sqrt(MLA_NOPE + MLA_ROPE)
DIL_QSCALE = LOG2E / math.sqrt(DIL_HEAD_DIM)
NEG_BIG = -1e30
MASK_DIST = 1e30
STAT_LANES = LANES // DIL_HEADS
ONES_ROWS = 16

TOKEN_TILE = 512
MLA_TQ = 256
MLA_TK = 512
VMEM_LIMIT = 56 * 1024 * 1024

NT_DIMS = (((1,), (1,)), ((), ()))


def _rms(x, g):
    return x * lax.rsqrt(jnp.mean(x * x, axis=-1, keepdims=True) + NORM_EPS) * g


def _const_spec(shape, single_buffer=False):
    index_map = lambda *_: (0,) * len(shape)
    if single_buffer:
        return pl.BlockSpec(shape, index_map, pipeline_mode=pl.Buffered(1))
    return pl.BlockSpec(shape, index_map)


def _params(n_axes):
    return pltpu.CompilerParams(dimension_semantics=("arbitrary",) * n_axes,
                                vmem_limit_bytes=VMEM_LIMIT)


def _rope(x, cos_t, sin_lo, sin_hi):
    return x * cos_t + pltpu.roll(x, LANES - 16, 1) * sin_lo + pltpu.roll(x, 16, 1) * sin_hi


def _pre_kernel(x_ref, tab_ref, g_pre_ref, w_in_ref, g_q_ref, w_uq_ref, g_kv_ref, w_k_ref, w_uvt_ref,
                q_ref, k_ref, vt_ref, dq_ref, dk_ref, dv_ref, *, tk):
    h = _rms(x_ref[0], g_pre_ref[...]).astype(BF16)
    z = jnp.dot(h, w_in_ref[...], preferred_element_type=F32)
    cos_t = tab_ref[:, 0:LANES]
    sin_lo = tab_ref[:, LANES:2 * LANES]
    sin_hi = tab_ref[:, 2 * LANES:3 * LANES]

    lat_q = _rms(z[:, :Q_LORA], g_q_ref[...]).astype(BF16)
    q = jnp.dot(lat_q, w_uq_ref[...], preferred_element_type=F32)
    for hd in range(MLA_HEADS):
        sl = slice(hd * HEAD_PAD, (hd + 1) * HEAD_PAD)
        q_ref[0, :, sl] = (_rope(q[:, sl], cos_t, sin_lo, sin_hi) * MLA_QSCALE).astype(BF16)

    lat_kv = _rms(z[:, Q_LORA:Q_LORA + KV_LORA], g_kv_ref[...])
    k_rope = _rope(z[:, Z_KROPE:Z_KROPE + LANES], cos_t, sin_lo, sin_hi)
    k_in = jnp.concatenate([lat_kv, k_rope], axis=1).astype(BF16)
    k_ref[0] = jnp.dot(k_in, w_k_ref[...], preferred_element_type=F32).astype(BF16)

    vt = lax.dot_general(w_uvt_ref[...], lat_kv.astype(BF16), NT_DIMS, preferred_element_type=F32)
    for j in range(vt_ref.shape[1]):
        vt_ref[0, j] = vt[:, j * tk:(j + 1) * tk].astype(BF16)

    dq_ref[0] = (z[:, Z_DIL:Z_DIL + DIL_W] * DIL_QSCALE).astype(BF16)
    dk_ref[0] = z[:, Z_DIL + DIL_W:Z_DIL + 2 * DIL_W].astype(BF16)
    dv_ref[0] = z[:, Z_DIL + 2 * DIL_W:Z_DIL + 3 * DIL_W].astype(BF16)


def _pre_call(x, tab, lw):
    B, S, D = x.shape
    ts, tk = TOKEN_TILE, MLA_TK
    tok = lambda w: pl.BlockSpec((1, ts, w), lambda b, i: (b, i, 0))
    return pl.pallas_call(
        functools.partial(_pre_kernel, tk=tk),
        grid=(B, S // ts),
        in_specs=[
            tok(D),
            pl.BlockSpec((ts, 3 * LANES), lambda b, i: (i, 0)),
            _const_spec((1, D)), _const_spec((D, Z_COLS)),
            _const_spec((1, Q_LORA)), _const_spec((Q_LORA, MLA_QK)),
            _const_spec((1, KV_LORA)), _const_spec((2 * LANES, MLA_QK)),
            _const_spec((MLA_OUT, KV_LORA)),
        ],
        out_specs=[
            tok(MLA_QK), tok(MLA_QK),
            pl.BlockSpec((1, ts // tk, MLA_OUT, tk), lambda b, i: (b, i, 0, 0)),
            tok(DIL_W), tok(DIL_W), tok(DIL_W),
        ],
        out_shape=[
            jax.ShapeDtypeStruct((B, S, MLA_QK), BF16),
            jax.ShapeDtypeStruct((B, S, MLA_QK), BF16),
            jax.ShapeDtypeStruct((B, S // tk, MLA_OUT, tk), BF16),
            jax.ShapeDtypeStruct((B, S, DIL_W), BF16),
            jax.ShapeDtypeStruct((B, S, DIL_W), BF16),
            jax.ShapeDtypeStruct((B, S, DIL_W), BF16),
        ],
        compiler_params=_params(2),
        name="pre_mix",
    )(x, tab, lw["g_pre_mix"], lw["w_in"], lw["g_q_lat"], lw["w_uq"], lw["g_kv_lat"], lw["w_k"],
      lw["w_uvt"])


def _mla_kernel(q_ref, k_ref, vt_ref, o_ref):
    tq = q_ref.shape[1]
    nk, _, tk = vt_ref.shape[1:]
    ones = (lax.broadcasted_iota(jnp.int32, (ONES_ROWS, tk), 0) == 0).astype(BF16)
    outs = []
    for hh in range(2):
        qh = q_ref[0, :, hh * HEAD_PAD:(hh + 1) * HEAD_PAD]

        def body(t, carry, hh=hh, qh=qh):
            m, acc = carry
            off = pl.multiple_of(t * tk, tk)
            kt = k_ref[0, pl.ds(off, tk), hh * HEAD_PAD:(hh + 1) * HEAD_PAD]
            s = lax.dot_general(kt, qh, NT_DIMS, preferred_element_type=F32)
            m_new = jnp.maximum(m, jnp.max(s, axis=0, keepdims=True))
            alpha = jnp.exp2(m - m_new)
            p = jnp.exp2(s - m_new).astype(BF16)
            vt = vt_ref[0, t, hh * MLA_V:(hh + 1) * MLA_V, :]
            pv = jnp.dot(jnp.concatenate([vt, ones], axis=0), p, preferred_element_type=F32)
            return m_new, alpha * acc + pv

        m0 = jnp.full((1, tq), NEG_BIG, F32)
        acc0 = jnp.zeros((MLA_V + ONES_ROWS, tq), F32)
        _, acc = lax.fori_loop(0, nk, body, (m0, acc0))
        outs.append(acc[:MLA_V] / acc[MLA_V:MLA_V + 1])
    o_ref[0] = jnp.concatenate(outs, axis=0).T


def _mla_call(q, k, vt):
    B, S, _ = q.shape
    nk, _, tk = vt.shape[1:]
    tq = MLA_TQ
    pair = 2 * HEAD_PAD
    return pl.pallas_call(
        _mla_kernel,
        grid=(B, MLA_HEADS // 2, S // tq),
        in_specs=[
            pl.BlockSpec((1, tq, pair), lambda b, hp, i: (b, i, hp)),
            pl.BlockSpec((1, S, pair), lambda b, hp, i: (b, 0, hp)),
            pl.BlockSpec((1, nk, 2 * MLA_V, tk), lambda b, hp, i: (b, 0, hp, 0)),
        ],
        out_specs=pl.BlockSpec((1, tq, 2 * MLA_V), lambda b, hp, i: (b, i, hp)),
        out_shape=jax.ShapeDtypeStruct((B, S, MLA_OUT), F32),
        compiler_params=_params(3),
        name="mla_attn",
    )(q, k, vt)


def _dil_kernel(*refs, d, tq, win, first, last):
    q_ref, k_ref, v_ref = refs[:3]
    refs = refs[3:]
    if not first:
        acc_in, m_in, l_in = refs[:3]
        refs = refs[3:]
    if last:
        (o_ref,) = refs
    else:
        acc_out, m_out, l_out = refs

    r = pl.program_id(2)
    seq = k_ref.shape[1]
    side = DIL_PAIRS[0][0] // 2
    start = pl.multiple_of(jnp.clip(r * tq - side, 0, seq - win), side)
    kwin = k_ref[0, pl.ds(start, win), :]
    vwin = v_ref[0, pl.ds(start, win), :]
    qi = r * tq + lax.broadcasted_iota(jnp.int32, (tq, win), 0)
    kj = start + lax.broadcasted_iota(jnp.int32, (tq, win), 1)
    rel = jnp.abs(kj - qi)
    dist = jnp.where(rel <= side, rel.astype(F32), MASK_DIST)
    q = q_ref[0]
    for hd in range(DIL_HEADS):
        sl = slice(hd * DIL_HEAD_DIM, (hd + 1) * DIL_HEAD_DIM)
        st = slice(hd * STAT_LANES, (hd + 1) * STAT_LANES)
        slope = 2.0 ** (-8.0 * (hd + 1.0) / DIL_HEADS)
        s = lax.dot_general(q[:, sl], kwin[:, sl], NT_DIMS, preferred_element_type=F32)
        s = s - (slope * d * LOG2E) * dist
        m_blk = jnp.max(s, axis=1, keepdims=True)
        if first:
            m_new = m_blk
        else:
            m_prev = m_in[0, :, st][:, :1]
            m_new = jnp.maximum(m_prev, m_blk)
        p = jnp.exp2(s - m_new)
        l_new = jnp.sum(p, axis=1, keepdims=True)
        acc = jnp.dot(p.astype(BF16), vwin[:, sl], preferred_element_type=F32)
        if not first:
            alpha = jnp.exp2(m_prev - m_new)
            l_new = alpha * l_in[0, :, st][:, :1] + l_new
            acc = alpha * acc_in[0, :, sl] + acc
        if last:
            o_ref[0, :, sl] = acc / l_new
        else:
            acc_out[0, :, sl] = acc
            m_out[0, :, st] = jnp.broadcast_to(m_new, (tq, STAT_LANES))
            l_out[0, :, st] = jnp.broadcast_to(l_new, (tq, STAT_LANES))


def _dil_call(d, dq, dk, dv, state, first, last):
    B, S, _ = dq.shape
    seq = S // d
    side = DIL_PAIRS[0][0] // 2
    tq = 256 if seq >= 512 else 128
    win = min(tq + 2 * side, seq)
    assert seq % tq == 0
    fold = lambda t: t.reshape(B, seq, d * t.shape[-1])
    tile = lambda w: pl.BlockSpec((1, tq, w), lambda b, c, r: (b, r, c))
    kv = pl.BlockSpec((1, seq, DIL_W), lambda b, c, r: (b, 0, c))
    in_specs = [tile(DIL_W), kv, kv]
    args = [fold(dq), fold(dk), fold(dv)]
    if not first:
        in_specs += [tile(DIL_W), tile(LANES), tile(LANES)]
        args += [fold(t) for t in state]
    if last:
        out_specs = tile(DIL_W)
        out_shape = jax.ShapeDtypeStruct((B, seq, d * DIL_W), F32)
    else:
        out_specs = [tile(DIL_W), tile(LANES), tile(LANES)]
        out_shape = [jax.ShapeDtypeStruct((B, seq, d * DIL_W), F32),
                     jax.ShapeDtypeStruct((B, seq, d * LANES), F32),
                     jax.ShapeDtypeStruct((B, seq, d * LANES), F32)]
    out = pl.pallas_call(
        functools.partial(_dil_kernel, d=d, tq=tq, win=win, first=first, last=last),
        grid=(B, d, seq // tq),
        in_specs=in_specs,
        out_specs=out_specs,
        out_shape=out_shape,
        compiler_params=_params(3),
        name=f"dilated_d{d}",
    )(*args)
    if last:
        return out.reshape(B, S, DIL_W)
    return [t.reshape(B, S, t.shape[-1] // d) for t in out]


def _dilated(dq, dk, dv):
    state = None
    n = len(DIL_PAIRS)
    for j, (window, d) in enumerate(DIL_PAIRS):
        assert window // (2 * d) == DIL_PAIRS[0][0] // 2
        state = _dil_call(d, dq, dk, dv, state, first=(j == 0), last=(j == n - 1))
    return state


def _post_kernel(x_ref, a_ref, b_ref, g_a_ref, g_b_ref, w_out_ref, g_mix_ref, g_pre_ref, w_up_ref,
                 w_down_ref, g_post_ref, o_ref):
    an = _rms(a_ref[0], g_a_ref[...]).astype(BF16)
    bn = _rms(b_ref[0], g_b_ref[...]).astype(BF16)
    mix = jnp.dot(an, w_out_ref[:MLA_OUT, :], preferred_element_type=F32)
    mix = mix + jnp.dot(bn, w_out_ref[MLA_OUT:, :], preferred_element_type=F32)
    x1 = x_ref[0] + _rms(mix, g_mix_ref[...])
    h = _rms(x1, g_pre_ref[...]).astype(BF16)
    y = jnp.zeros_like(x1)
    for c in range(D_FF // D_MODEL):
        cs = slice(c * D_MODEL, (c + 1) * D_MODEL)
        u = jnp.dot(h, w_up_ref[:, cs], preferred_element_type=F32)
        u = jnp.square(jnp.maximum(u, 0.0)).astype(BF16)
        y = y + jnp.dot(u, w_down_ref[cs, :], preferred_element_type=F32)
    o_ref[0] = x1 + _rms(y, g_post_ref[...])


def _post_call(x, a, b, lw):
    B, S, D = x.shape
    ts = TOKEN_TILE
    tok = lambda w: pl.BlockSpec((1, ts, w), lambda bi, i: (bi, i, 0))
    return pl.pallas_call(
        _post_kernel,
        grid=(B, S // ts),
        in_specs=[
            tok(D), tok(MLA_OUT), tok(DIL_W),
            _const_spec((1, MLA_OUT)), _const_spec((1, DIL_W)),
            _const_spec((D, D), True), _const_spec((1, D)), _const_spec((1, D)),
            _const_spec((D, D_FF), True), _const_spec((D_FF, D), True), _const_spec((1, D)),
        ],
        out_specs=tok(D),
        out_shape=jax.ShapeDtypeStruct((B, S, D), F32),
        compiler_params=_params(2),
        name="post_mix_mlp",
    )(x, a, b, lw["g_out_mla"], lw["g_out_dil"], lw["w_out"], lw["g_post_mix"], lw["g_pre_mlp"],
      lw["w_up"], lw["w_down"], lw["g_post_mlp"])


def _rope_table(S):
    half = MLA_ROPE // 2
    inv_freq = ROPE_THETA ** (-jnp.arange(half, dtype=F32) / half)
    ang = jnp.arange(S, dtype=F32)[:, None] * inv_freq[None, :]
    cos, sin = jnp.cos(ang), jnp.sin(ang)
    ones = jnp.ones((S, MLA_NOPE), F32)
    zh = jnp.zeros((S, half), F32)
    z32 = jnp.zeros((S, LANES - MLA_NOPE - MLA_ROPE), F32)
    z64 = jnp.zeros((S, MLA_NOPE), F32)
    cos_t = jnp.concatenate([ones, cos, cos, z32], axis=1)
    sin_lo = jnp.concatenate([z64, -sin, zh, z32], axis=1)
    sin_hi = jnp.concatenate([z64, zh, sin, z32], axis=1)
    return jnp.concatenate([cos_t, sin_lo, sin_hi], axis=1)


def _layer_weights(l, g_pre_mix, w_in, g_q_lat, w_uq, g_kv_lat, w_ukv, g_out_mla, g_out_dil, w_out,
                   g_post_mix, g_pre_mlp, w_up, w_down, g_post_mlp):
    row = lambda g: g[l][None, :].astype(F32)
    wi = w_in[l]
    c1 = Q_LORA + KV_LORA
    c2 = c1 + MLA_ROPE
    pad = lambda n: jnp.zeros((D_MODEL, n), wi.dtype)
    w_in_p = jnp.concatenate(
        [wi[:, :c1], pad(MLA_NOPE), wi[:, c1:c2], pad(LANES - MLA_NOPE - MLA_ROPE), wi[:, c2:]],
        axis=1)
    w_uq_p = jnp.pad(w_uq[l], ((0, 0), (0, 0), (0, HEAD_PAD - MLA_NOPE - MLA_ROPE)))
    w_uk_p = jnp.pad(w_ukv[l][:, :, :MLA_NOPE], ((0, 0), (0, 0), (0, HEAD_PAD - MLA_NOPE)))
    place = jnp.zeros((LANES, MLA_HEADS, HEAD_PAD), F32)
    idx = MLA_NOPE + jnp.arange(MLA_ROPE)
    place = place.at[idx, :, idx].set(1.0)
    w_k = jnp.concatenate([w_uk_p, place], axis=0).reshape(2 * LANES, MLA_QK)
    w_uv = w_ukv[l][:, :, MLA_NOPE:].reshape(KV_LORA, MLA_OUT)
    return {
        "g_pre_mix": row(g_pre_mix), "w_in": w_in_p.astype(BF16),
        "g_q_lat": row(g_q_lat), "w_uq": w_uq_p.reshape(Q_LORA, MLA_QK).astype(BF16),
        "g_kv_lat": row(g_kv_lat), "w_k": w_k.astype(BF16), "w_uvt": w_uv.T.astype(BF16),
        "g_out_mla": row(g_out_mla), "g_out_dil": row(g_out_dil), "w_out": w_out[l].astype(BF16),
        "g_post_mix": row(g_post_mix), "g_pre_mlp": row(g_pre_mlp), "w_up": w_up[l].astype(BF16),
        "w_down": w_down[l].astype(BF16), "g_post_mlp": row(g_post_mlp),
    }


def _layer(x, tab, lw):
    q, k, vt, dq, dk, dv = _pre_call(x, tab, lw)
    a = _mla_call(q, k, vt)
    b = _dilated(dq, dk, dv)
    return _post_call(x, a, b, lw)


def kernel(x_prompt, x_sample, g_pre_mix, w_in, g_q_lat, w_uq, g_kv_lat, w_ukv, g_out_mla, g_out_dil,
           w_out, g_post_mix, g_pre_mlp, w_up, w_down, g_post_mlp):
    depth = w_in.shape[0]
    layers = [_layer_weights(l, g_pre_mix, w_in, g_q_lat, w_uq, g_kv_lat, w_ukv, g_out_mla,
                             g_out_dil, w_out, g_post_mix, g_pre_mlp, w_up, w_down, g_post_mlp)
              for l in range(depth)]

    def trunk(x):
        tab = _rope_table(x.shape[1])
        for lw in layers:
            x = _layer(x, tab, lw)
        return x

    return (trunk(x_prompt), trunk(x_sample))
```

```python
import functools
import math

import jax
import jax.numpy as jnp
from jax import lax
from jax.experimental import pallas as pl
from jax.experimental.pallas import tpu as pltpu

F32 = jnp.float32
BF16 = jnp.bfloat16

D_MODEL = 1024
MLA_HEADS = 8
MLA_NOPE = 64
MLA_ROPE = 32
MLA_V = 64
Q_LORA = 256
KV_LORA = 128
ROPE_THETA = 10000.0
DIL_HEADS = 8
DIL_HEAD_DIM = 64
DIL_PAIRS = ((128, 1), (512, 4), (2048, 16))
D_FF = 4 * D_MODEL
NORM_EPS = 1e-6

LANES = 128
HEAD_PAD = LANES
MLA_QK = MLA_HEADS * HEAD_PAD
MLA_OUT = MLA_HEADS * MLA_V
DIL_W = DIL_HEADS * DIL_HEAD_DIM
Z_COLS = 2048
Z_KROPE = Q_LORA + KV_LORA
Z_DIL = 512
LOG2E = 1.4426950408889634
MLA_QSCALE = LOG2E / math.sqrt(MLA_NOPE + MLA_ROPE)
DIL_QSCALE = LOG2E / math.sqrt(DIL_HEAD_DIM)
NEG_BIG = -1e30
MASK_DIST = 1e30
STAT_LANES = LANES // DIL_HEADS
ONES_ROWS = 16

TOKEN_TILE = 512
MLA_TQ = 256
MLA_TK = 512
VMEM_LIMIT = 56 * 1024 * 1024

NT_DIMS = (((1,), (1,)), ((), ()))


def _rms(x, g):
    return x * lax.rsqrt(jnp.mean(x * x, axis=-1, keepdims=True) + NORM_EPS) * g


def _const_spec(shape, single_buffer=False):
    index_map = lambda *_: (0,) * len(shape)
    if single_buffer:
        return pl.BlockSpec(shape, index_map, pipeline_mode=pl.Buffered(1))
    return pl.BlockSpec(shape, index_map)


def _params(n_axes):
    return pltpu.CompilerParams(dimension_semantics=("arbitrary",) * n_axes,
                                vmem_limit_bytes=VMEM_LIMIT)


def _rope(x, cos_t, sin_lo, sin_hi):
    return x * cos_t + pltpu.roll(x, LANES - 16, 1) * sin_lo + pltpu.roll(x, 16, 1) * sin_hi


def _pre_kernel(x_ref, tab_ref, g_pre_ref, w_in_ref, g_q_ref, w_uq_ref, g_kv_ref, w_k_ref, w_uvt_ref,
                q_ref, k_ref, vt_ref, dq_ref, dk_ref, dv_ref, *, tk):
    h = _rms(x_ref[0], g_pre_ref[...]).astype(BF16)
    z = jnp.dot(h, w_in_ref[...], preferred_element_type=F32)
    cos_t = tab_ref[:, 0:LANES]
    sin_lo = tab_ref[:, LANES:2 * LANES]
    sin_hi = tab_ref[:, 2 * LANES:3 * LANES]

    lat_q = _rms(z[:, :Q_LORA], g_q_ref[...]).astype(BF16)
    q = jnp.dot(lat_q, w_uq_ref[...], preferred_element_type=F32)
    for hd in range(MLA_HEADS):
        sl = slice(hd * HEAD_PAD, (hd + 1) * HEAD_PAD)
        q_ref[0, :, sl] = (_rope(q[:, sl], cos_t, sin_lo, sin_hi) * MLA_QSCALE).astype(BF16)

    lat_kv = _rms(z[:, Q_LORA:Q_LORA + KV_LORA], g_kv_ref[...])
    k_rope = _rope(z[:, Z_KROPE:Z_KROPE + LANES], cos_t, sin_lo, sin_hi)
    k_in = jnp.concatenate([lat_kv, k_rope], axis=1).astype(BF16)
    k_ref[0] = jnp.dot(k_in, w_k_ref[...], preferred_element_type=F32).astype(BF16)

    vt = lax.dot_general(w_uvt_ref[...], lat_kv.astype(BF16), NT_DIMS, preferred_element_type=F32)
    for j in range(vt_ref.shape[1]):
        vt_ref[0, j] = vt[:, j * tk:(j + 1) * tk].astype(BF16)

    dq_ref[0] = (z[:, Z_DIL:Z_DIL + DIL_W] * DIL_QSCALE).astype(BF16)
    dk_ref[0] = z[:, Z_DIL + DIL_W:Z_DIL + 2 * DIL_W].astype(BF16)
    dv_ref[0] = z[:, Z_DIL + 2 * DIL_W:Z_DIL + 3 * DIL_W].astype(BF16)


def _pre_call(x, tab, lw):
    B, S, D = x.shape
    ts, tk = TOKEN_TILE, MLA_TK
    tok = lambda w: pl.BlockSpec((1, ts, w), lambda b, i: (b, i, 0))
    return pl.pallas_call(
        functools.partial(_pre_kernel, tk=tk),
        grid=(B, S // ts),
        in_specs=[
            tok(D),
            pl.BlockSpec((ts, 3 * LANES), lambda b, i: (i, 0)),
            _const_spec((1, D)), _const_spec((D, Z_COLS)),
            _const_spec((1, Q_LORA)), _const_spec((Q_LORA, MLA_QK)),
            _const_spec((1, KV_LORA)), _const_spec((2 * LANES, MLA_QK)),
            _const_spec((MLA_OUT, KV_LORA)),
        ],
        out_specs=[
            tok(MLA_QK), tok(MLA_QK),
            pl.BlockSpec((1, ts // tk, MLA_OUT, tk), lambda b, i: (b, i, 0, 0)),
            tok(DIL_W), tok(DIL_W), tok(DIL_W),
        ],
        out_shape=[
            jax.ShapeDtypeStruct((B, S, MLA_QK), BF16),
            jax.ShapeDtypeStruct((B, S, MLA_QK), BF16),
            jax.ShapeDtypeStruct((B, S // tk, MLA_OUT, tk), BF16),
            jax.ShapeDtypeStruct((B, S, DIL_W), BF16),
            jax.ShapeDtypeStruct((B, S, DIL_W), BF16),
            jax.ShapeDtypeStruct((B, S, DIL_W), BF16),
        ],
        compiler_params=_params(2),
        name="pre_mix",
    )(x, tab, lw["g_pre_mix"], lw["w_in"], lw["g_q_lat"], lw["w_uq"], lw["g_kv_lat"], lw["w_k"],
      lw["w_uvt"])


def _mla_kernel(q_ref, k_ref, vt_ref, o_ref, s_ref, acc_ref):
    tq = q_ref.shape[1]
    nk, _, tk = vt_ref.shape[1:]
    ones = (lax.broadcasted_iota(jnp.int32, (ONES_ROWS, tk), 0) == 0).astype(BF16)
    heads = range(2)
    qs = [q_ref[0, :, hh * HEAD_PAD:(hh + 1) * HEAD_PAD] for hh in heads]

    def scores(t, slot):
        off = pl.multiple_of(t * tk, tk)
        for hh in heads:
            kt = k_ref[0, pl.ds(off, tk), hh * HEAD_PAD:(hh + 1) * HEAD_PAD]
            s_ref[slot, hh] = lax.dot_general(kt, qs[hh], NT_DIMS, preferred_element_type=F32)

    def attend(t, slot, ms):
        new = []
        for hh in heads:
            s = s_ref[slot, hh]
            m_new = jnp.maximum(ms[hh], jnp.max(s, axis=0, keepdims=True))
            alpha = jnp.exp2(ms[hh] - m_new)
            p = jnp.exp2(s - m_new).astype(BF16)
            vt = vt_ref[0, t, hh * MLA_V:(hh + 1) * MLA_V, :]
            pv = jnp.dot(jnp.concatenate([vt, ones], axis=0), p, preferred_element_type=F32)
            acc_ref[hh] = alpha * acc_ref[hh] + pv
            new.append(m_new)
        return tuple(new)

    def pair(u, ms, prefetch_next_pair=True):
        scores(2 * u + 1, 1)
        ms = attend(2 * u, 0, ms)
        if prefetch_next_pair:
            scores(2 * u + 2, 0)
        return attend(2 * u + 1, 1, ms)

    acc_ref[...] = jnp.zeros_like(acc_ref)
    scores(0, 0)
    ms = (jnp.full((1, tq), NEG_BIG, F32),) * 2
    ms = lax.fori_loop(0, nk // 2 - 1, pair, ms)
    pair(nk // 2 - 1, ms, prefetch_next_pair=False)
    outs = [acc_ref[hh, :MLA_V] / acc_ref[hh, MLA_V:MLA_V + 1] for hh in heads]
    o_ref[0] = jnp.concatenate(outs, axis=0).T


def _mla_call(q, k, vt):
    B, S, _ = q.shape
    nk, _, tk = vt.shape[1:]
    tq = MLA_TQ
    pair = 2 * HEAD_PAD
    return pl.pallas_call(
        _mla_kernel,
        grid=(B, MLA_HEADS // 2, S // tq),
        in_specs=[
            pl.BlockSpec((1, tq, pair), lambda b, hp, i: (b, i, hp)),
            pl.BlockSpec((1, S, pair), lambda b, hp, i: (b, 0, hp)),
            pl.BlockSpec((1, nk, 2 * MLA_V, tk), lambda b, hp, i: (b, 0, hp, 0)),
        ],
        out_specs=pl.BlockSpec((1, tq, 2 * MLA_V), lambda b, hp, i: (b, i, hp)),
        out_shape=jax.ShapeDtypeStruct((B, S, MLA_OUT), F32),
        scratch_shapes=[pltpu.VMEM((2, 2, tk, tq), F32),
                        pltpu.VMEM((2, MLA_V + ONES_ROWS, tq), F32)],
        compiler_params=_params(3),
        name="mla_attn",
    )(q, k, vt)


def _dil_kernel(*refs, d, tq, win, first, last):
    q_ref, k_ref, v_ref = refs[:3]
    refs = refs[3:]
    if not first:
        acc_in, m_in, l_in = refs[:3]
        refs = refs[3:]
    if last:
        (o_ref,) = refs
    else:
        acc_out, m_out, l_out = refs

    r = pl.program_id(2)
    seq = k_ref.shape[1]
    side = DIL_PAIRS[0][0] // 2
    start = pl.multiple_of(jnp.clip(r * tq - side, 0, seq - win), side)
    kwin = k_ref[0, pl.ds(start, win), :]
    vwin = v_ref[0, pl.ds(start, win), :]
    qi = r * tq + lax.broadcasted_iota(jnp.int32, (tq, win), 0)
    kj = start + lax.broadcasted_iota(jnp.int32, (tq, win), 1)
    rel = jnp.abs(kj - qi)
    dist = jnp.where(rel <= side, rel.astype(F32), MASK_DIST)
    q = q_ref[0]
    for hd in range(DIL_HEADS):
        sl = slice(hd * DIL_HEAD_DIM, (hd + 1) * DIL_HEAD_DIM)
        st = slice(hd * STAT_LANES, (hd + 1) * STAT_LANES)
        slope = 2.0 ** (-8.0 * (hd + 1.0) / DIL_HEADS)
        s = lax.dot_general(q[:, sl], kwin[:, sl], NT_DIMS, preferred_element_type=F32)
        s = s - (slope * d * LOG2E) * dist
        m_blk = jnp.max(s, axis=1, keepdims=True)
        if first:
            m_new = m_blk
        else:
            m_prev = m_in[0, :, st][:, :1]
            m_new = jnp.maximum(m_prev, m_blk)
        p = jnp.exp2(s - m_new)
        l_new = jnp.sum(p, axis=1, keepdims=True)
        acc = jnp.dot(p.astype(BF16), vwin[:, sl], preferred_element_type=F32)
        if not first:
            alpha = jnp.exp2(m_prev - m_new)
            l_new = alpha * l_in[0, :, st][:, :1] + l_new
            acc = alpha * acc_in[0, :, sl] + acc
        if last:
            o_ref[0, :, sl] = acc / l_new
        else:
            acc_out[0, :, sl] = acc
            m_out[0, :, st] = jnp.broadcast_to(m_new, (tq, STAT_LANES))
            l_out[0, :, st] = jnp.broadcast_to(l_new, (tq, STAT_LANES))


def _dil_call(d, dq, dk, dv, state, first, last):
    B, S, _ = dq.shape
    seq = S // d
    side = DIL_PAIRS[0][0] // 2
    tq = 256 if seq >= 512 else 128
    win = min(tq + 2 * side, seq)
    assert seq % tq == 0
    fold = lambda t: t.reshape(B, seq, d * t.shape[-1])
    tile = lambda w: pl.BlockSpec((1, tq, w), lambda b, c, r: (b, r, c))
    kv = pl.BlockSpec((1, seq, DIL_W), lambda b, c, r: (b, 0, c))
    in_specs = [tile(DIL_W), kv, kv]
    args = [fold(dq), fold(dk), fold(dv)]
    if not first:
        in_specs += [tile(DIL_W), tile(LANES), tile(LANES)]
        args += [fold(t) for t in state]
    if last:
        out_specs = tile(DIL_W)
        out_shape = jax.ShapeDtypeStruct((B, seq, d * DIL_W), F32)
    else:
        out_specs = [tile(DIL_W), tile(LANES), tile(LANES)]
        out_shape = [jax.ShapeDtypeStruct((B, seq, d * DIL_W), F32),
                     jax.ShapeDtypeStruct((B, seq, d * LANES), F32),
                     jax.ShapeDtypeStruct((B, seq, d * LANES), F32)]
    out = pl.pallas_call(
        functools.partial(_dil_kernel, d=d, tq=tq, win=win, first=first, last=last),
        grid=(B, d, seq // tq),
        in_specs=in_specs,
        out_specs=out_specs,
        out_shape=out_shape,
        compiler_params=_params(3),
        name=f"dilated_d{d}",
    )(*args)
    if last:
        return out.reshape(B, S, DIL_W)
    return [t.reshape(B, S, t.shape[-1] // d) for t in out]


def _dilated(dq, dk, dv):
    state = None
    n = len(DIL_PAIRS)
    for j, (window, d) in enumerate(DIL_PAIRS):
        assert window // (2 * d) == DIL_PAIRS[0][0] // 2
        state = _dil_call(d, dq, dk, dv, state, first=(j == 0), last=(j == n - 1))
    return state


def _post_kernel(x_ref, a_ref, b_ref, g_a_ref, g_b_ref, w_out_ref, g_mix_ref, g_pre_ref, w_up_ref,
                 w_down_ref, g_post_ref, o_ref):
    an = _rms(a_ref[0], g_a_ref[...]).astype(BF16)
    bn = _rms(b_ref[0], g_b_ref[...]).astype(BF16)
    mix = jnp.dot(an, w_out_ref[:MLA_OUT, :], preferred_element_type=F32)
    mix = mix + jnp.dot(bn, w_out_ref[MLA_OUT:, :], preferred_element_type=F32)
    x1 = x_ref[0] + _rms(mix, g_mix_ref[...])
    h = _rms(x1, g_pre_ref[...]).astype(BF16)
    y = jnp.zeros_like(x1)
    for c in range(D_FF // D_MODEL):
        cs = slice(c * D_MODEL, (c + 1) * D_MODEL)
        u = jnp.dot(h, w_up_ref[:, cs], preferred_element_type=F32)
        u = jnp.square(jnp.maximum(u, 0.0)).astype(BF16)
        y = y + jnp.dot(u, w_down_ref[cs, :], preferred_element_type=F32)
    o_ref[0] = x1 + _rms(y, g_post_ref[...])


def _post_call(x, a, b, lw):
    B, S, D = x.shape
    ts = TOKEN_TILE
    tok = lambda w: pl.BlockSpec((1, ts, w), lambda bi, i: (bi, i, 0))
    return pl.pallas_call(
        _post_kernel,
        grid=(B, S // ts),
        in_specs=[
            tok(D), tok(MLA_OUT), tok(DIL_W),
            _const_spec((1, MLA_OUT)), _const_spec((1, DIL_W)),
            _const_spec((D, D), True), _const_spec((1, D)), _const_spec((1, D)),
            _const_spec((D, D_FF), True), _const_spec((D_FF, D), True), _const_spec((1, D)),
        ],
        out_specs=tok(D),
        out_shape=jax.ShapeDtypeStruct((B, S, D), F32),
        compiler_params=_params(2),
        name="post_mix_mlp",
    )(x, a, b, lw["g_out_mla"], lw["g_out_dil"], lw["w_out"], lw["g_post_mix"], lw["g_pre_mlp"],
      lw["w_up"], lw["w_down"], lw["g_post_mlp"])


def _rope_table(S):
    half = MLA_ROPE // 2
    inv_freq = ROPE_THETA ** (-jnp.arange(half, dtype=F32) / half)
    ang = jnp.arange(S, dtype=F32)[:, None] * inv_freq[None, :]
    cos, sin = jnp.cos(ang), jnp.sin(ang)
    ones = jnp.ones((S, MLA_NOPE), F32)
    zh = jnp.zeros((S, half), F32)
    z32 = jnp.zeros((S, LANES - MLA_NOPE - MLA_ROPE), F32)
    z64 = jnp.zeros((S, MLA_NOPE), F32)
    cos_t = jnp.concatenate([ones, cos, cos, z32], axis=1)
    sin_lo = jnp.concatenate([z64, -sin, zh, z32], axis=1)
    sin_hi = jnp.concatenate([z64, zh, sin, z32], axis=1)
    return jnp.concatenate([cos_t, sin_lo, sin_hi], axis=1)


def _layer_weights(l, g_pre_mix, w_in, g_q_lat, w_uq, g_kv_lat, w_ukv, g_out_mla, g_out_dil, w_out,
                   g_post_mix, g_pre_mlp, w_up, w_down, g_post_mlp):
    row = lambda g: g[l][None, :].astype(F32)
    wi = w_in[l]
    c1 = Q_LORA + KV_LORA
    c2 = c1 + MLA_ROPE
    pad = lambda n: jnp.zeros((D_MODEL, n), wi.dtype)
    w_in_p = jnp.concatenate(
        [wi[:, :c1], pad(MLA_NOPE), wi[:, c1:c2], pad(LANES - MLA_NOPE - MLA_ROPE), wi[:, c2:]],
        axis=1)
    w_uq_p = jnp.pad(w_uq[l], ((0, 0), (0, 0), (0, HEAD_PAD - MLA_NOPE - MLA_ROPE)))
    w_uk_p = jnp.pad(w_ukv[l][:, :, :MLA_NOPE], ((0, 0), (0, 0), (0, HEAD_PAD - MLA_NOPE)))
    place = jnp.zeros((LANES, MLA_HEADS, HEAD_PAD), F32)
    idx = MLA_NOPE + jnp.arange(MLA_ROPE)
    place = place.at[idx, :, idx].set(1.0)
    w_k = jnp.concatenate([w_uk_p, place], axis=0).reshape(2 * LANES, MLA_QK)
    w_uv = w_ukv[l][:, :, MLA_NOPE:].reshape(KV_LORA, MLA_OUT)
    return {
        "g_pre_mix": row(g_pre_mix), "w_in": w_in_p.astype(BF16),
        "g_q_lat": row(g_q_lat), "w_uq": w_uq_p.reshape(Q_LORA, MLA_QK).astype(BF16),
        "g_kv_lat": row(g_kv_lat), "w_k": w_k.astype(BF16), "w_uvt": w_uv.T.astype(BF16),
        "g_out_mla": row(g_out_mla), "g_out_dil": row(g_out_dil), "w_out": w_out[l].astype(BF16),
        "g_post_mix": row(g_post_mix), "g_pre_mlp": row(g_pre_mlp), "w_up": w_up[l].astype(BF16),
        "w_down": w_down[l].astype(BF16), "g_post_mlp": row(g_post_mlp),
    }


def _layer(x, tab, lw):
    q, k, vt, dq, dk, dv = _pre_call(x, tab, lw)
    a = _mla_call(q, k, vt)
    b = _dilated(dq, dk, dv)
    return _post_call(x, a, b, lw)


def kernel(x_prompt, x_sample, g_pre_mix, w_in, g_q_lat, w_uq, g_kv_lat, w_ukv, g_out_mla, g_out_dil,
           w_out, g_post_mix, g_pre_mlp, w_up, w_down, g_post_mlp):
    depth = w_in.shape[0]
    layers = [_layer_weights(l, g_pre_mix, w_in, g_q_lat, w_uq, g_kv_lat, w_ukv, g_out_mla,
                             g_out_dil, w_out, g_post_mix, g_pre_mlp, w_up, w_down, g_post_mlp)
              for l in range(depth)]

    def trunk(x):
        tab = _rope_table(x.shape[1])
        for lw in layers:
            x = _layer(x, tab, lw)
        return x

    return (trunk(x_prompt), trunk(x_sample))
```

```python
import functools
import math

import jax
import jax.numpy as jnp
from jax import lax
from jax.experimental import pallas as pl
from jax.experimental.pallas import tpu as pltpu

F32 = jnp.float32
BF16 = jnp.bfloat16

D_MODEL = 1024
MLA_HEADS = 8
MLA_NOPE = 64
MLA_ROPE = 32
MLA_V = 64
Q_LORA = 256
KV_LORA = 128
ROPE_THETA = 10000.0
DIL_HEADS = 8
DIL_HEAD_DIM = 64
DIL_PAIRS = ((128, 1), (512, 4), (2048, 16))
D_FF = 4 * D_MODEL
NORM_EPS = 1e-6

LANES = 128
HEAD_PAD = LANES
MLA_QK = MLA_HEADS * HEAD_PAD
MLA_OUT = MLA_HEADS * MLA_V
DIL_W = DIL_HEADS * DIL_HEAD_DIM
Z_COLS = 2048
Z_KROPE = Q_LORA + KV_LORA
Z_DIL = 512
LOG2E = 1.4426950408889634
MLA_QSCALE = LOG2E / math.sqrt(MLA_NOPE + MLA_ROPE)
DIL_QSCALE = LOG2E / math.sqrt(DIL_HEAD_DIM)
NEG_BIG = -1e30
MASK_DIST = 1e30
DIL_SIDE = DIL_PAIRS[0][0] // (2 * DIL_PAIRS[0][1])
DIL_STEP = DIL_PAIRS[1][1] // DIL_PAIRS[0][1]
DIL_TQ = 128
ONES_ROWS = 16

TOKEN_TILE = 512
MLA_TQ = 256
MLA_TK = 512
VMEM_LIMIT = 56 * 1024 * 1024

NT_DIMS = (((1,), (1,)), ((), ()))


def _rms(x, g):
    return x * lax.rsqrt(jnp.mean(x * x, axis=-1, keepdims=True) + NORM_EPS) * g


def _const_spec(shape, single_buffer=False):
    index_map = lambda *_: (0,) * len(shape)
    if single_buffer:
        return pl.BlockSpec(shape, index_map, pipeline_mode=pl.Buffered(1))
    return pl.BlockSpec(shape, index_map)


def _params(n_axes):
    return pltpu.CompilerParams(dimension_semantics=("arbitrary",) * n_axes,
                                vmem_limit_bytes=VMEM_LIMIT)


def _rope(x, cos_t, sin_lo, sin_hi):
    return x * cos_t + pltpu.roll(x, LANES - 16, 1) * sin_lo + pltpu.roll(x, 16, 1) * sin_hi


def _pre_kernel(x_ref, tab_ref, g_pre_ref, w_in_ref, g_q_ref, w_uq_ref, g_kv_ref, w_k_ref, w_uvt_ref,
                q_ref, k_ref, vt_ref, *rest, tk):
    dil_refs, zd_ref = rest[:-1], rest[-1]
    h = _rms(x_ref[0], g_pre_ref[...]).astype(BF16)
    z = jnp.dot(h, w_in_ref[...], preferred_element_type=F32)
    cos_t = tab_ref[:, 0:LANES]
    sin_lo = tab_ref[:, LANES:2 * LANES]
    sin_hi = tab_ref[:, 2 * LANES:3 * LANES]

    lat_q = _rms(z[:, :Q_LORA], g_q_ref[...]).astype(BF16)
    q = jnp.dot(lat_q, w_uq_ref[...], preferred_element_type=F32)
    for hd in range(MLA_HEADS):
        sl = slice(hd * HEAD_PAD, (hd + 1) * HEAD_PAD)
        q_ref[0, :, sl] = (_rope(q[:, sl], cos_t, sin_lo, sin_hi) * MLA_QSCALE).astype(BF16)

    lat_kv = _rms(z[:, Q_LORA:Q_LORA + KV_LORA], g_kv_ref[...])
    k_rope = _rope(z[:, Z_KROPE:Z_KROPE + LANES], cos_t, sin_lo, sin_hi)
    k_in = jnp.concatenate([lat_kv, k_rope], axis=1).astype(BF16)
    k_ref[0] = jnp.dot(k_in, w_k_ref[...], preferred_element_type=F32).astype(BF16)

    vt = lax.dot_general(w_uvt_ref[...], lat_kv.astype(BF16), NT_DIMS, preferred_element_type=F32)
    for j in range(vt_ref.shape[1]):
        vt_ref[0, j] = vt[:, j * tk:(j + 1) * tk].astype(BF16)

    n_blk, ts, _ = zd_ref.shape
    for j in range(n_blk):
        blk = z[:, Z_DIL + j * LANES:Z_DIL + (j + 1) * LANES]
        zd_ref[j] = blk * DIL_QSCALE if j * LANES < DIL_W else blk
    for (_, d), ref in zip(DIL_PAIRS, dil_refs):
        for c in range(d):
            rows = pl.ds(c, ts // d, stride=d) if d > 1 else slice(None)
            for j in range(n_blk):
                ref[0, c, :, j * LANES:(j + 1) * LANES] = zd_ref[j, rows, :].astype(BF16)


def _pre_call(x, tab, lw):
    B, S, D = x.shape
    ts, tk = TOKEN_TILE, MLA_TK
    tok = lambda w: pl.BlockSpec((1, ts, w), lambda b, i: (b, i, 0))
    return pl.pallas_call(
        functools.partial(_pre_kernel, tk=tk),
        grid=(B, S // ts),
        in_specs=[
            tok(D),
            pl.BlockSpec((ts, 3 * LANES), lambda b, i: (i, 0)),
            _const_spec((1, D)), _const_spec((D, Z_COLS)),
            _const_spec((1, Q_LORA)), _const_spec((Q_LORA, MLA_QK)),
            _const_spec((1, KV_LORA)), _const_spec((2 * LANES, MLA_QK)),
            _const_spec((MLA_OUT, KV_LORA)),
        ],
        out_specs=[
            tok(MLA_QK), tok(MLA_QK),
            pl.BlockSpec((1, ts // tk, MLA_OUT, tk), lambda b, i: (b, i, 0, 0)),
        ] + [pl.BlockSpec((1, d, ts // d, 3 * DIL_W), lambda b, i: (b, 0, i, 0)) for _, d in DIL_PAIRS],
        out_shape=[
            jax.ShapeDtypeStruct((B, S, MLA_QK), BF16),
            jax.ShapeDtypeStruct((B, S, MLA_QK), BF16),
            jax.ShapeDtypeStruct((B, S // tk, MLA_OUT, tk), BF16),
        ] + [jax.ShapeDtypeStruct((B, d, S // d, 3 * DIL_W), BF16) for _, d in DIL_PAIRS],
        scratch_shapes=[pltpu.VMEM((3 * DIL_W // LANES, ts, LANES), F32)],
        compiler_params=_params(2),
        name="pre_mix",
    )(x, tab, lw["g_pre_mix"], lw["w_in"], lw["g_q_lat"], lw["w_uq"], lw["g_kv_lat"], lw["w_k"],
      lw["w_uvt"])


def _mla_kernel(q_ref, k_ref, vt_ref, o_ref, s_ref, acc_ref):
    tq = q_ref.shape[1]
    nk, _, tk = vt_ref.shape[1:]
    ones = (lax.broadcasted_iota(jnp.int32, (ONES_ROWS, tk), 0) == 0).astype(BF16)
    heads = range(2)
    qs = [q_ref[0, :, hh * HEAD_PAD:(hh + 1) * HEAD_PAD] for hh in heads]

    def scores(t, slot):
        off = pl.multiple_of(t * tk, tk)
        for hh in heads:
            kt = k_ref[0, pl.ds(off, tk), hh * HEAD_PAD:(hh + 1) * HEAD_PAD]
            s_ref[slot, hh] = lax.dot_general(kt, qs[hh], NT_DIMS, preferred_element_type=F32)

    def attend(t, slot, ms):
        new = []
        for hh in heads:
            s = s_ref[slot, hh]
            m_new = jnp.maximum(ms[hh], jnp.max(s, axis=0, keepdims=True))
            alpha = jnp.exp2(ms[hh] - m_new)
            p = jnp.exp2(s - m_new).astype(BF16)
            vt = vt_ref[0, t, hh * MLA_V:(hh + 1) * MLA_V, :]
            pv = jnp.dot(jnp.concatenate([vt, ones], axis=0), p, preferred_element_type=F32)
            acc_ref[hh] = alpha * acc_ref[hh] + pv
            new.append(m_new)
        return tuple(new)

    def pair(u, ms, prefetch_next_pair=True):
        scores(2 * u + 1, 1)
        ms = attend(2 * u, 0, ms)
        if prefetch_next_pair:
            scores(2 * u + 2, 0)
        return attend(2 * u + 1, 1, ms)

    acc_ref[...] = jnp.zeros_like(acc_ref)
    scores(0, 0)
    ms = (jnp.full((1, tq), NEG_BIG, F32),) * 2
    ms = lax.fori_loop(0, nk // 2 - 1, pair, ms)
    pair(nk // 2 - 1, ms, prefetch_next_pair=False)
    outs = [acc_ref[hh, :MLA_V] / acc_ref[hh, MLA_V:MLA_V + 1] for hh in heads]
    o_ref[0] = jnp.concatenate(outs, axis=0).T


def _mla_call(q, k, vt):
    B, S, _ = q.shape
    nk, _, tk = vt.shape[1:]
    tq = MLA_TQ
    pair = 2 * HEAD_PAD
    return pl.pallas_call(
        _mla_kernel,
        grid=(B, MLA_HEADS // 2, S // tq),
        in_specs=[
            pl.BlockSpec((1, tq, pair), lambda b, hp, i: (b, i, hp)),
            pl.BlockSpec((1, S, pair), lambda b, hp, i: (b, 0, hp)),
            pl.BlockSpec((1, nk, 2 * MLA_V, tk), lambda b, hp, i: (b, 0, hp, 0)),
        ],
        out_specs=pl.BlockSpec((1, tq, 2 * MLA_V), lambda b, hp, i: (b, i, hp)),
        out_shape=jax.ShapeDtypeStruct((B, S, MLA_OUT), F32),
        scratch_shapes=[pltpu.VMEM((2, 2, tk, tq), F32),
                        pltpu.VMEM((2, MLA_V + ONES_ROWS, tq), F32)],
        compiler_params=_params(3),
        name="mla_attn",
    )(q, k, vt)


def _dil_kernel(*refs, d, tq, win, first, last):
    q_ref, k_ref, v_ref = refs[:3]
    refs = refs[3:]
    if not first:
        acc_in, st_in = refs[:2]
        refs = refs[2:]
    n_out = 1 if last else 2
    outs, (s_ref, acc_scr, st_scr) = refs[:n_out], refs[n_out:]

    r = pl.program_id(2)
    seq = k_ref.shape[2]
    start = pl.multiple_of(jnp.clip(r * tq - DIL_SIDE, 0, seq - win), DIL_SIDE)
    kwin = k_ref[0, 0, pl.ds(start, win), :]
    vt = v_ref[0, 0, pl.ds(start, win), :].T
    q = q_ref[0, 0]
    heads = range(DIL_HEADS)
    per_blk = LANES // DIL_HEAD_DIM
    lane_head = lax.broadcasted_iota(jnp.int32, (tq, LANES), 1) // DIL_HEAD_DIM
    for hd in heads:
        cs = slice(hd // per_blk * LANES, (hd // per_blk + 1) * LANES)
        q_hd = jnp.where(lane_head == hd % per_blk, q[:, cs], jnp.zeros_like(q[:, cs]))
        s_ref[hd] = lax.dot_general(kwin[:, cs], q_hd, NT_DIMS, preferred_element_type=F32)
    kj = start + lax.broadcasted_iota(jnp.int32, (win, tq), 0)
    qi = r * tq + lax.broadcasted_iota(jnp.int32, (win, tq), 1)
    rel = jnp.abs(kj - qi)
    dist = jnp.where(rel <= DIL_SIDE, rel.astype(F32), MASK_DIST)
    if not first:
        acc_prev = acc_in[0, 0].T
        st_prev = st_in[0, 0].T
    ones = (lax.broadcasted_iota(jnp.int32, (ONES_ROWS, win), 0) == 0).astype(BF16)
    accs, ms, ls = [], [], []
    for hd in heads:
        rows = slice(hd * DIL_HEAD_DIM, (hd + 1) * DIL_HEAD_DIM)
        slope = 2.0 ** (-8.0 * (hd + 1.0) / DIL_HEADS)
        s = s_ref[hd] - (slope * d * LOG2E) * dist
        m_new = jnp.max(s, axis=0, keepdims=True)
        if not first:
            m_prev = st_prev[hd:hd + 1]
            m_new = jnp.maximum(m_prev, m_new)
        p = jnp.exp2(s - m_new).astype(BF16)
        pv = jnp.dot(jnp.concatenate([vt[rows], ones], axis=0), p, preferred_element_type=F32)
        acc, l_new = pv[:DIL_HEAD_DIM], pv[DIL_HEAD_DIM:DIL_HEAD_DIM + 1]
        if not first:
            alpha = jnp.exp2(m_prev - m_new)
            l_new = alpha * st_prev[DIL_HEADS + hd:DIL_HEADS + hd + 1] + l_new
            acc = alpha * acc_prev[rows] + acc
        if last:
            acc = acc / l_new
        accs.append(acc)
        ms.append(m_new)
        ls.append(l_new)
    out_tile = jnp.concatenate(accs, axis=0).T
    if last:
        outs[0][0, 0] = out_tile
    else:
        pad = jnp.zeros((LANES - 2 * DIL_HEADS, tq), F32)
        st_scr[0] = jnp.concatenate(ms + ls + [pad], axis=0).T
        for j in range(acc_scr.shape[0]):
            acc_scr[j] = out_tile[:, j * LANES:(j + 1) * LANES]
        for scr, out in zip((acc_scr, st_scr), outs):
            for k in range(DIL_STEP):
                rows = pl.ds(k, tq // DIL_STEP, stride=DIL_STEP)
                for j in range(scr.shape[0]):
                    out[0, k, 0, :, j * LANES:(j + 1) * LANES] = scr[j, rows, :]


def _dil_call(d, qkv, state, first, last):
    B, _, seq, _ = qkv.shape
    tq = DIL_TQ
    win = min(tq + 2 * DIL_SIDE, seq)
    assert seq % tq == 0
    tile = lambda w: pl.BlockSpec((1, 1, tq, w), lambda b, c, r: (b, c, r, 0))
    kv = lambda j: pl.BlockSpec((1, 1, seq, DIL_W), lambda b, c, r: (b, c, 0, j))
    in_specs = [tile(DIL_W), kv(1), kv(2)]
    args = [qkv, qkv, qkv]
    if not first:
        in_specs += [tile(DIL_W), tile(LANES)]
        args += list(state)
    if last:
        out_specs = tile(DIL_W)
        out_shape = jax.ShapeDtypeStruct((B, d, seq, DIL_W), F32)
    else:
        refold = lambda w: pl.BlockSpec((1, DIL_STEP, 1, tq // DIL_STEP, w),
                                        lambda b, c, r: (b, 0, c, r, 0))
        widths = (DIL_W, LANES)
        out_specs = [refold(w) for w in widths]
        out_shape = [jax.ShapeDtypeStruct((B, DIL_STEP, d, seq // DIL_STEP, w), F32) for w in widths]
    out = pl.pallas_call(
        functools.partial(_dil_kernel, d=d, tq=tq, win=win, first=first, last=last),
        grid=(B, d, seq // tq),
        in_specs=in_specs,
        out_specs=out_specs,
        out_shape=out_shape,
        scratch_shapes=[pltpu.VMEM((DIL_HEADS, win, tq), F32),
                        pltpu.VMEM((DIL_W // LANES, tq, LANES), F32),
                        pltpu.VMEM((1, tq, LANES), F32)],
        compiler_params=_params(3),
        name=f"dilated_d{d}",
    )(*args)
    if last:
        return out
    return [t.reshape(B, DIL_STEP * d, seq // DIL_STEP, t.shape[-1]) for t in out]


def _dilated(qkvs):
    state = None
    n = len(DIL_PAIRS)
    for j, ((window, d), qkv) in enumerate(zip(DIL_PAIRS, qkvs)):
        assert window // (2 * d) == DIL_SIDE and (j == 0 or d == DIL_STEP * DIL_PAIRS[j - 1][1])
        state = _dil_call(d, qkv, state, first=(j == 0), last=(j == n - 1))
    return state


def _post_kernel(x_ref, a_ref, b_ref, g_a_ref, g_b_ref, w_out_ref, g_mix_ref, g_pre_ref, w_up_ref,
                 w_down_ref, g_post_ref, o_ref, b_scr):
    d_last = b_ref.shape[1]
    for c in range(d_last):
        rows = pl.ds(c, b_ref.shape[2], stride=d_last)
        for j in range(b_scr.shape[0]):
            b_scr[j, rows, :] = b_ref[0, c, :, j * LANES:(j + 1) * LANES]
    b = jnp.concatenate([b_scr[j] for j in range(b_scr.shape[0])], axis=1)
    an = _rms(a_ref[0], g_a_ref[...]).astype(BF16)
    bn = _rms(b, g_b_ref[...]).astype(BF16)
    mix = jnp.dot(an, w_out_ref[:MLA_OUT, :], preferred_element_type=F32)
    mix = mix + jnp.dot(bn, w_out_ref[MLA_OUT:, :], preferred_element_type=F32)
    x1 = x_ref[0] + _rms(mix, g_mix_ref[...])
    h = _rms(x1, g_pre_ref[...]).astype(BF16)
    y = jnp.zeros_like(x1)
    for c in range(D_FF // D_MODEL):
        cs = slice(c * D_MODEL, (c + 1) * D_MODEL)
        u = jnp.dot(h, w_up_ref[:, cs], preferred_element_type=F32)
        u = jnp.square(jnp.maximum(u, 0.0)).astype(BF16)
        y = y + jnp.dot(u, w_down_ref[cs, :], preferred_element_type=F32)
    o_ref[0] = x1 + _rms(y, g_post_ref[...])


def _post_call(x, a, b, lw):
    B, S, D = x.shape
    ts = TOKEN_TILE
    d_last = b.shape[1]
    tok = lambda w: pl.BlockSpec((1, ts, w), lambda bi, i: (bi, i, 0))
    return pl.pallas_call(
        _post_kernel,
        grid=(B, S // ts),
        in_specs=[
            tok(D), tok(MLA_OUT),
            pl.BlockSpec((1, d_last, ts // d_last, DIL_W), lambda bi, i: (bi, 0, i, 0)),
            _const_spec((1, MLA_OUT)), _const_spec((1, DIL_W)),
            _const_spec((D, D), True), _const_spec((1, D)), _const_spec((1, D)),
            _const_spec((D, D_FF), True), _const_spec((D_FF, D), True), _const_spec((1, D)),
        ],
        out_specs=tok(D),
        out_shape=jax.ShapeDtypeStruct((B, S, D), F32),
        scratch_shapes=[pltpu.VMEM((DIL_W // LANES, ts, LANES), F32)],
        compiler_params=_params(2),
        name="post_mix_mlp",
    )(x, a, b, lw["g_out_mla"], lw["g_out_dil"], lw["w_out"], lw["g_post_mix"], lw["g_pre_mlp"],
      lw["w_up"], lw["w_down"], lw["g_post_mlp"])


def _rope_table(S):
    half = MLA_ROPE // 2
    inv_freq = ROPE_THETA ** (-jnp.arange(half, dtype=F32) / half)
    ang = jnp.arange(S, dtype=F32)[:, None] * inv_freq[None, :]
    cos, sin = jnp.cos(ang), jnp.sin(ang)
    ones = jnp.ones((S, MLA_NOPE), F32)
    zh = jnp.zeros((S, half), F32)
    z32 = jnp.zeros((S, LANES - MLA_NOPE - MLA_ROPE), F32)
    z64 = jnp.zeros((S, MLA_NOPE), F32)
    cos_t = jnp.concatenate([ones, cos, cos, z32], axis=1)
    sin_lo = jnp.concatenate([z64, -sin, zh, z32], axis=1)
    sin_hi = jnp.concatenate([z64, zh, sin, z32], axis=1)
    return jnp.concatenate([cos_t, sin_lo, sin_hi], axis=1)


def _layer_weights(l, g_pre_mix, w_in, g_q_lat, w_uq, g_kv_lat, w_ukv, g_out_mla, g_out_dil, w_out,
                   g_post_mix, g_pre_mlp, w_up, w_down, g_post_mlp):
    row = lambda g: g[l][None, :].astype(F32)
    wi = w_in[l]
    c1 = Q_LORA + KV_LORA
    c2 = c1 + MLA_ROPE
    pad = lambda n: jnp.zeros((D_MODEL, n), wi.dtype)
    w_in_p = jnp.concatenate(
        [wi[:, :c1], pad(MLA_NOPE), wi[:, c1:c2], pad(LANES - MLA_NOPE - MLA_ROPE), wi[:, c2:]],
        axis=1)
    w_uq_p = jnp.pad(w_uq[l], ((0, 0), (0, 0), (0, HEAD_PAD - MLA_NOPE - MLA_ROPE)))
    w_uk_p = jnp.pad(w_ukv[l][:, :, :MLA_NOPE], ((0, 0), (0, 0), (0, HEAD_PAD - MLA_NOPE)))
    place = jnp.zeros((LANES, MLA_HEADS, HEAD_PAD), F32)
    idx = MLA_NOPE + jnp.arange(MLA_ROPE)
    place = place.at[idx, :, idx].set(1.0)
    w_k = jnp.concatenate([w_uk_p, place], axis=0).reshape(2 * LANES, MLA_QK)
    w_uv = w_ukv[l][:, :, MLA_NOPE:].reshape(KV_LORA, MLA_OUT)
    return {
        "g_pre_mix": row(g_pre_mix), "w_in": w_in_p.astype(BF16),
        "g_q_lat": row(g_q_lat), "w_uq": w_uq_p.reshape(Q_LORA, MLA_QK).astype(BF16),
        "g_kv_lat": row(g_kv_lat), "w_k": w_k.astype(BF16), "w_uvt": w_uv.T.astype(BF16),
        "g_out_mla": row(g_out_mla), "g_out_dil": row(g_out_dil), "w_out": w_out[l].astype(BF16),
        "g_post_mix": row(g_post_mix), "g_pre_mlp": row(g_pre_mlp), "w_up": w_up[l].astype(BF16),
        "w_down": w_down[l].astype(BF16), "g_post_mlp": row(g_post_mlp),
    }


def _layer(x, tab, lw):
    q, k, vt, *qkvs = _pre_call(x, tab, lw)
    a = _mla_call(q, k, vt)
    b = _dilated(qkvs)
    return _post_call(x, a, b, lw)


def kernel(x_prompt, x_sample, g_pre_mix, w_in, g_q_lat, w_uq, g_kv_lat, w_ukv, g_out_mla, g_out_dil,
           w_out, g_post_mix, g_pre_mlp, w_up, w_down, g_post_mlp):
    depth = w_in.shape[0]
    layers = [_layer_weights(l, g_pre_mix, w_in, g_q_lat, w_uq, g_kv_lat, w_ukv, g_out_mla,
                             g_out_dil, w_out, g_post_mix, g_pre_mlp, w_up, w_down, g_post_mlp)
              for l in range(depth)]

    def trunk(x):
        tab = _rope_table(x.shape[1])
        for lw in layers:
            x = _layer(x, tab, lw)
        return x

    return (trunk(x_prompt), trunk(x_sample))
```

```python
import functools
import math

import jax
import jax.numpy as jnp
from jax import lax
from jax.experimental import pallas as pl
from jax.experimental.pallas import tpu as pltpu

F32 = jnp.float32
BF16 = jnp.bfloat16

D_MODEL = 1024
MLA_HEADS = 8
MLA_NOPE = 64
MLA_ROPE = 32
MLA_V = 64
Q_LORA = 256
KV_LORA = 128
ROPE_THETA = 10000.0
DIL_HEADS = 8
DIL_HEAD_DIM = 64
DIL_PAIRS = ((128, 1), (512, 4), (2048, 16))
D_FF = 4 * D_MODEL
NORM_EPS = 1e-6

LANES = 128
HEAD_PAD = LANES
MLA_QK = MLA_HEADS * HEAD_PAD
MLA_OUT = MLA_HEADS * MLA_V
DIL_W = DIL_HEADS * DIL_HEAD_DIM
Z_COLS = 2048
Z_KROPE = Q_LORA + KV_LORA
Z_DIL = 512
LOG2E = 1.4426950408889634
MLA_QSCALE = LOG2E / math.sqrt(MLA_NOPE + MLA_ROPE)
DIL_QSCALE = LOG2E / math.sqrt(DIL_HEAD_DIM)
NEG_BIG = -1e30
MASK_DIST = 1e30
DIL_SIDE = DIL_PAIRS[0][0] // (2 * DIL_PAIRS[0][1])
DIL_STEP = DIL_PAIRS[1][1] // DIL_PAIRS[0][1]
DIL_TQ = 128
DIL_STEP_ROWS = 512
ONES_ROWS = 16

TOKEN_TILE = 512
MLA_TQ = 512
MLA_TK = 512
VMEM_LIMIT = 56 * 1024 * 1024

NT_DIMS = (((1,), (1,)), ((), ()))


def _rms(x, g):
    return x * lax.rsqrt(jnp.mean(x * x, axis=-1, keepdims=True) + NORM_EPS) * g


def _const_spec(shape, single_buffer=False):
    index_map = lambda *_: (0,) * len(shape)
    if single_buffer:
        return pl.BlockSpec(shape, index_map, pipeline_mode=pl.Buffered(1))
    return pl.BlockSpec(shape, index_map)


def _params(n_axes):
    return pltpu.CompilerParams(dimension_semantics=("arbitrary",) * n_axes,
                                vmem_limit_bytes=VMEM_LIMIT)


def _rope(x, cos_t, sin_lo, sin_hi):
    return x * cos_t + pltpu.roll(x, LANES - 16, 1) * sin_lo + pltpu.roll(x, 16, 1) * sin_hi


def _pre_kernel(x_ref, tab_ref, g_pre_ref, w_in_ref, g_q_ref, w_uq_ref, g_kv_ref, w_k_ref, w_uvt_ref,
                q_ref, k_ref, vt_ref, *rest, tk):
    dil_refs, zd_ref = rest[:-1], rest[-1]
    h = _rms(x_ref[0], g_pre_ref[...]).astype(BF16)
    z = jnp.dot(h, w_in_ref[...], preferred_element_type=F32)
    cos_t = tab_ref[:, 0:LANES]
    sin_lo = tab_ref[:, LANES:2 * LANES]
    sin_hi = tab_ref[:, 2 * LANES:3 * LANES]

    lat_q = _rms(z[:, :Q_LORA], g_q_ref[...]).astype(BF16)
    q = jnp.dot(lat_q, w_uq_ref[...], preferred_element_type=F32)
    for hd in range(MLA_HEADS):
        sl = slice(hd * HEAD_PAD, (hd + 1) * HEAD_PAD)
        q_ref[0, :, sl] = (_rope(q[:, sl], cos_t, sin_lo, sin_hi) * MLA_QSCALE).astype(BF16)

    lat_kv = _rms(z[:, Q_LORA:Q_LORA + KV_LORA], g_kv_ref[...])
    k_rope = _rope(z[:, Z_KROPE:Z_KROPE + LANES], cos_t, sin_lo, sin_hi)
    k_in = jnp.concatenate([lat_kv, k_rope], axis=1).astype(BF16)
    k_ref[0] = jnp.dot(k_in, w_k_ref[...], preferred_element_type=F32).astype(BF16)

    vt = lax.dot_general(w_uvt_ref[...], lat_kv.astype(BF16), NT_DIMS, preferred_element_type=F32)
    for j in range(vt_ref.shape[1]):
        vt_ref[0, j] = vt[:, j * tk:(j + 1) * tk].astype(BF16)

    n_blk, ts, _ = zd_ref.shape
    for j in range(n_blk):
        blk = z[:, Z_DIL + j * LANES:Z_DIL + (j + 1) * LANES]
        zd_ref[j] = blk * DIL_QSCALE if j * LANES < DIL_W else blk
    for (_, d), ref in zip(DIL_PAIRS, dil_refs):
        for c in range(d):
            rows = pl.ds(c, ts // d, stride=d) if d > 1 else slice(None)
            for j in range(n_blk):
                ref[0, c, :, j * LANES:(j + 1) * LANES] = zd_ref[j, rows, :].astype(BF16)


def _pre_call(x, tab, lw):
    B, S, D = x.shape
    ts, tk = TOKEN_TILE, MLA_TK
    tok = lambda w: pl.BlockSpec((1, ts, w), lambda b, i: (b, i, 0))
    return pl.pallas_call(
        functools.partial(_pre_kernel, tk=tk),
        grid=(B, S // ts),
        in_specs=[
            tok(D),
            pl.BlockSpec((ts, 3 * LANES), lambda b, i: (i, 0)),
            _const_spec((1, D)), _const_spec((D, Z_COLS)),
            _const_spec((1, Q_LORA)), _const_spec((Q_LORA, MLA_QK)),
            _const_spec((1, KV_LORA)), _const_spec((2 * LANES, MLA_QK)),
            _const_spec((MLA_OUT, KV_LORA)),
        ],
        out_specs=[
            tok(MLA_QK), tok(MLA_QK),
            pl.BlockSpec((1, ts // tk, MLA_OUT, tk), lambda b, i: (b, i, 0, 0)),
        ] + [pl.BlockSpec((1, d, ts // d, 3 * DIL_W), lambda b, i: (b, 0, i, 0)) for _, d in DIL_PAIRS],
        out_shape=[
            jax.ShapeDtypeStruct((B, S, MLA_QK), BF16),
            jax.ShapeDtypeStruct((B, S, MLA_QK), BF16),
            jax.ShapeDtypeStruct((B, S // tk, MLA_OUT, tk), BF16),
        ] + [jax.ShapeDtypeStruct((B, d, S // d, 3 * DIL_W), BF16) for _, d in DIL_PAIRS],
        scratch_shapes=[pltpu.VMEM((3 * DIL_W // LANES, ts, LANES), F32)],
        compiler_params=_params(2),
        name="pre_mix",
    )(x, tab, lw["g_pre_mix"], lw["w_in"], lw["g_q_lat"], lw["w_uq"], lw["g_kv_lat"], lw["w_k"],
      lw["w_uvt"])


def _mla_kernel(q_ref, k_ref, vt_ref, o_ref, s_ref, acc_ref):
    tq = q_ref.shape[1]
    nk, _, tk = vt_ref.shape[1:]
    ones = (lax.broadcasted_iota(jnp.int32, (ONES_ROWS, tk), 0) == 0).astype(BF16)
    heads = range(2)
    qs = [q_ref[0, :, hh * HEAD_PAD:(hh + 1) * HEAD_PAD] for hh in heads]

    def scores(t, slot):
        off = pl.multiple_of(t * tk, tk)
        for hh in heads:
            kt = k_ref[0, pl.ds(off, tk), hh * HEAD_PAD:(hh + 1) * HEAD_PAD]
            s_ref[slot, hh] = lax.dot_general(kt, qs[hh], NT_DIMS, preferred_element_type=F32)

    def attend(t, slot, ms):
        new = []
        for hh in heads:
            s = s_ref[slot, hh]
            m_new = jnp.maximum(ms[hh], jnp.max(s, axis=0, keepdims=True))
            alpha = jnp.exp2(ms[hh] - m_new)
            p = jnp.exp2(s - m_new).astype(BF16)
            vt = vt_ref[0, t, hh * MLA_V:(hh + 1) * MLA_V, :]
            pv = jnp.dot(jnp.concatenate([vt, ones], axis=0), p, preferred_element_type=F32)
            acc_ref[hh] = alpha * acc_ref[hh] + pv
            new.append(m_new)
        return tuple(new)

    def pair(u, ms, prefetch_next_pair=True):
        scores(2 * u + 1, 1)
        ms = attend(2 * u, 0, ms)
        if prefetch_next_pair:
            scores(2 * u + 2, 0)
        return attend(2 * u + 1, 1, ms)

    acc_ref[...] = jnp.zeros_like(acc_ref)
    scores(0, 0)
    ms = (jnp.full((1, tq), NEG_BIG, F32),) * 2
    ms = lax.fori_loop(0, nk // 2 - 1, pair, ms)
    pair(nk // 2 - 1, ms, prefetch_next_pair=False)
    outs = [acc_ref[hh, :MLA_V] / acc_ref[hh, MLA_V:MLA_V + 1] for hh in heads]
    o_ref[0] = jnp.concatenate(outs, axis=0).T


def _mla_call(q, k, vt):
    B, S, _ = q.shape
    nk, _, tk = vt.shape[1:]
    tq = MLA_TQ
    pair = 2 * HEAD_PAD
    return pl.pallas_call(
        _mla_kernel,
        grid=(B, MLA_HEADS // 2, S // tq),
        in_specs=[
            pl.BlockSpec((1, tq, pair), lambda b, hp, i: (b, i, hp)),
            pl.BlockSpec((1, S, pair), lambda b, hp, i: (b, 0, hp)),
            pl.BlockSpec((1, nk, 2 * MLA_V, tk), lambda b, hp, i: (b, 0, hp, 0)),
        ],
        out_specs=pl.BlockSpec((1, tq, 2 * MLA_V), lambda b, hp, i: (b, i, hp)),
        out_shape=jax.ShapeDtypeStruct((B, S, MLA_OUT), F32),
        scratch_shapes=[pltpu.VMEM((2, 2, tk, tq), F32),
                        pltpu.VMEM((2, MLA_V + ONES_ROWS, tq), F32)],
        compiler_params=_params(3),
        name="mla_attn",
    )(q, k, vt)


def _dil_kernel(*refs, d, tq, win, first, last):
    q_ref, k_ref, v_ref = refs[:3]
    refs = refs[3:]
    if not first:
        acc_in, st_in = refs[:2]
        refs = refs[2:]
    n_out = 1 if last else 2
    outs, (s_ref, acc_scr, st_scr) = refs[:n_out], refs[n_out:]

    r = pl.program_id(2)
    seq = k_ref.shape[2]
    n_sub = q_ref.shape[2] // tq
    heads = range(DIL_HEADS)
    per_blk = LANES // DIL_HEAD_DIM
    lane_head = lax.broadcasted_iota(jnp.int32, (tq, LANES), 1) // DIL_HEAD_DIM
    ones = (lax.broadcasted_iota(jnp.int32, (ONES_ROWS, win), 0) == 0).astype(BF16)

    def window_start(sub):
        t0 = (r * n_sub + sub) * tq
        return t0, pl.multiple_of(jnp.clip(t0 - DIL_SIDE, 0, seq - win), DIL_SIDE)

    def scores(sub):
        _, start = window_start(sub)
        kwin = k_ref[0, 0, pl.ds(start, win), :]
        q = q_ref[0, 0, sub * tq:(sub + 1) * tq, :]
        for hd in heads:
            cs = slice(hd // per_blk * LANES, (hd // per_blk + 1) * LANES)
            q_hd = jnp.where(lane_head == hd % per_blk, q[:, cs], jnp.zeros_like(q[:, cs]))
            s_ref[sub, hd] = lax.dot_general(kwin[:, cs], q_hd, NT_DIMS, preferred_element_type=F32)

    def attend(sub):
        t0, start = window_start(sub)
        qrows = slice(sub * tq, (sub + 1) * tq)
        vt = v_ref[0, 0, pl.ds(start, win), :].T
        kj = start + lax.broadcasted_iota(jnp.int32, (win, tq), 0)
        qi = t0 + lax.broadcasted_iota(jnp.int32, (win, tq), 1)
        rel = jnp.abs(kj - qi)
        dist = jnp.where(rel <= DIL_SIDE, rel.astype(F32), MASK_DIST)
        if not first:
            acc_prev = acc_in[0, 0, qrows, :].T
            st_prev = st_in[0, 0, qrows, :].T
        accs, ms, ls = [], [], []
        for hd in heads:
            rows = slice(hd * DIL_HEAD_DIM, (hd + 1) * DIL_HEAD_DIM)
            slope = 2.0 ** (-8.0 * (hd + 1.0) / DIL_HEADS)
            s = s_ref[sub, hd] - (slope * d * LOG2E) * dist
            m_new = jnp.max(s, axis=0, keepdims=True)
            if not first:
                m_prev = st_prev[hd:hd + 1]
                m_new = jnp.maximum(m_prev, m_new)
            p = jnp.exp2(s - m_new).astype(BF16)
            pv = jnp.dot(jnp.concatenate([vt[rows], ones], axis=0), p, preferred_element_type=F32)
            acc, l_new = pv[:DIL_HEAD_DIM], pv[DIL_HEAD_DIM:DIL_HEAD_DIM + 1]
            if not first:
                alpha = jnp.exp2(m_prev - m_new)
                l_new = alpha * st_prev[DIL_HEADS + hd:DIL_HEADS + hd + 1] + l_new
                acc = alpha * acc_prev[rows] + acc
            if last:
                acc = acc / l_new
            accs.append(acc)
            ms.append(m_new)
            ls.append(l_new)
        out_tile = jnp.concatenate(accs, axis=0).T
        if last:
            outs[0][0, 0, qrows, :] = out_tile
        else:
            pad = jnp.zeros((LANES - 2 * DIL_HEADS, tq), F32)
            st_scr[0, qrows, :] = jnp.concatenate(ms + ls + [pad], axis=0).T
            for j in range(acc_scr.shape[0]):
                acc_scr[j, qrows, :] = out_tile[:, j * LANES:(j + 1) * LANES]

    scores(0)
    for sub in range(n_sub):
        if sub + 1 < n_sub:
            scores(sub + 1)
        attend(sub)
    if not last:
        for scr, out in zip((acc_scr, st_scr), outs):
            for k in range(DIL_STEP):
                rows = pl.ds(k, n_sub * tq // DIL_STEP, stride=DIL_STEP)
                for j in range(scr.shape[0]):
                    out[0, k, 0, :, j * LANES:(j + 1) * LANES] = scr[j, rows, :]


def _dil_call(d, qkv, state, first, last):
    B, _, seq, _ = qkv.shape
    win = min(DIL_TQ + 2 * DIL_SIDE, seq)
    tq = min(DIL_STEP_ROWS, seq)
    assert seq % tq == 0 and tq % DIL_TQ == 0
    tile = lambda w: pl.BlockSpec((1, 1, tq, w), lambda b, c, r: (b, c, r, 0))
    kv = lambda j: pl.BlockSpec((1, 1, seq, DIL_W), lambda b, c, r: (b, c, 0, j))
    in_specs = [tile(DIL_W), kv(1), kv(2)]
    args = [qkv, qkv, qkv]
    if not first:
        in_specs += [tile(DIL_W), tile(LANES)]
        args += list(state)
    if last:
        out_specs = tile(DIL_W)
        out_shape = jax.ShapeDtypeStruct((B, d, seq, DIL_W), F32)
    else:
        refold = lambda w: pl.BlockSpec((1, DIL_STEP, 1, tq // DIL_STEP, w),
                                        lambda b, c, r: (b, 0, c, r, 0))
        widths = (DIL_W, LANES)
        out_specs = [refold(w) for w in widths]
        out_shape = [jax.ShapeDtypeStruct((B, DIL_STEP, d, seq // DIL_STEP, w), F32) for w in widths]
    out = pl.pallas_call(
        functools.partial(_dil_kernel, d=d, tq=DIL_TQ, win=win, first=first, last=last),
        grid=(B, d, seq // tq),
        in_specs=in_specs,
        out_specs=out_specs,
        out_shape=out_shape,
        scratch_shapes=[pltpu.VMEM((tq // DIL_TQ, DIL_HEADS, win, DIL_TQ), F32),
                        pltpu.VMEM((DIL_W // LANES, tq, LANES), F32),
                        pltpu.VMEM((1, tq, LANES), F32)],
        compiler_params=_params(3),
        name=f"dilated_d{d}",
    )(*args)
    if last:
        return out
    return [t.reshape(B, DIL_STEP * d, seq // DIL_STEP, t.shape[-1]) for t in out]


def _dilated(qkvs):
    state = None
    n = len(DIL_PAIRS)
    for j, ((window, d), qkv) in enumerate(zip(DIL_PAIRS, qkvs)):
        assert window // (2 * d) == DIL_SIDE and (j == 0 or d == DIL_STEP * DIL_PAIRS[j - 1][1])
        state = _dil_call(d, qkv, state, first=(j == 0), last=(j == n - 1))
    return state


def _post_kernel(x_ref, a_ref, b_ref, g_a_ref, g_b_ref, w_out_ref, g_mix_ref, g_pre_ref, w_up_ref,
                 w_down_ref, g_post_ref, o_ref, b_scr):
    d_last = b_ref.shape[1]
    for c in range(d_last):
        rows = pl.ds(c, b_ref.shape[2], stride=d_last)
        for j in range(b_scr.shape[0]):
            b_scr[j, rows, :] = b_ref[0, c, :, j * LANES:(j + 1) * LANES]
    b = jnp.concatenate([b_scr[j] for j in range(b_scr.shape[0])], axis=1)
    an = _rms(a_ref[0], g_a_ref[...]).astype(BF16)
    bn = _rms(b, g_b_ref[...]).astype(BF16)
    mix = jnp.dot(an, w_out_ref[:MLA_OUT, :], preferred_element_type=F32)
    mix = mix + jnp.dot(bn, w_out_ref[MLA_OUT:, :], preferred_element_type=F32)
    x1 = x_ref[0] + _rms(mix, g_mix_ref[...])
    h = _rms(x1, g_pre_ref[...]).astype(BF16)
    y = jnp.zeros_like(x1)
    for c in range(D_FF // D_MODEL):
        cs = slice(c * D_MODEL, (c + 1) * D_MODEL)
        u = jnp.dot(h, w_up_ref[:, cs], preferred_element_type=F32)
        u = jnp.square(jnp.maximum(u, 0.0)).astype(BF16)
        y = y + jnp.dot(u, w_down_ref[cs, :], preferred_element_type=F32)
    o_ref[0] = x1 + _rms(y, g_post_ref[...])


def _post_call(x, a, b, lw):
    B, S, D = x.shape
    ts = TOKEN_TILE
    d_last = b.shape[1]
    tok = lambda w: pl.BlockSpec((1, ts, w), lambda bi, i: (bi, i, 0))
    return pl.pallas_call(
        _post_kernel,
        grid=(B, S // ts),
        in_specs=[
            tok(D), tok(MLA_OUT),
            pl.BlockSpec((1, d_last, ts // d_last, DIL_W), lambda bi, i: (bi, 0, i, 0)),
            _const_spec((1, MLA_OUT)), _const_spec((1, DIL_W)),
            _const_spec((D, D), True), _const_spec((1, D)), _const_spec((1, D)),
            _const_spec((D, D_FF), True), _const_spec((D_FF, D), True), _const_spec((1, D)),
        ],
        out_specs=tok(D),
        out_shape=jax.ShapeDtypeStruct((B, S, D), F32),
        scratch_shapes=[pltpu.VMEM((DIL_W // LANES, ts, LANES), F32)],
        compiler_params=_params(2),
        name="post_mix_mlp",
    )(x, a, b, lw["g_out_mla"], lw["g_out_dil"], lw["w_out"], lw["g_post_mix"], lw["g_pre_mlp"],
      lw["w_up"], lw["w_down"], lw["g_post_mlp"])


def _rope_table(S):
    half = MLA_ROPE // 2
    inv_freq = ROPE_THETA ** (-jnp.arange(half, dtype=F32) / half)
    ang = jnp.arange(S, dtype=F32)[:, None] * inv_freq[None, :]
    cos, sin = jnp.cos(ang), jnp.sin(ang)
    ones = jnp.ones((S, MLA_NOPE), F32)
    zh = jnp.zeros((S, half), F32)
    z32 = jnp.zeros((S, LANES - MLA_NOPE - MLA_ROPE), F32)
    z64 = jnp.zeros((S, MLA_NOPE), F32)
    cos_t = jnp.concatenate([ones, cos, cos, z32], axis=1)
    sin_lo = jnp.concatenate([z64, -sin, zh, z32], axis=1)
    sin_hi = jnp.concatenate([z64, zh, sin, z32], axis=1)
    return jnp.concatenate([cos_t, sin_lo, sin_hi], axis=1)


def _layer_weights(l, g_pre_mix, w_in, g_q_lat, w_uq, g_kv_lat, w_ukv, g_out_mla, g_out_dil, w_out,
                   g_post_mix, g_pre_mlp, w_up, w_down, g_post_mlp):
    row = lambda g: g[l][None, :].astype(F32)
    wi = w_in[l]
    c1 = Q_LORA + KV_LORA
    c2 = c1 + MLA_ROPE
    pad = lambda n: jnp.zeros((D_MODEL, n), wi.dtype)
    w_in_p = jnp.concatenate(
        [wi[:, :c1], pad(MLA_NOPE), wi[:, c1:c2], pad(LANES - MLA_NOPE - MLA_ROPE), wi[:, c2:]],
        axis=1)
    w_uq_p = jnp.pad(w_uq[l], ((0, 0), (0, 0), (0, HEAD_PAD - MLA_NOPE - MLA_ROPE)))
    w_uk_p = jnp.pad(w_ukv[l][:, :, :MLA_NOPE], ((0, 0), (0, 0), (0, HEAD_PAD - MLA_NOPE)))
    place = jnp.zeros((LANES, MLA_HEADS, HEAD_PAD), F32)
    idx = MLA_NOPE + jnp.arange(MLA_ROPE)
    place = place.at[idx, :, idx].set(1.0)
    w_k = jnp.concatenate([w_uk_p, place], axis=0).reshape(2 * LANES, MLA_QK)
    w_uv = w_ukv[l][:, :, MLA_NOPE:].reshape(KV_LORA, MLA_OUT)
    return {
        "g_pre_mix": row(g_pre_mix), "w_in": w_in_p.astype(BF16),
        "g_q_lat": row(g_q_lat), "w_uq": w_uq_p.reshape(Q_LORA, MLA_QK).astype(BF16),
        "g_kv_lat": row(g_kv_lat), "w_k": w_k.astype(BF16), "w_uvt": w_uv.T.astype(BF16),
        "g_out_mla": row(g_out_mla), "g_out_dil": row(g_out_dil), "w_out": w_out[l].astype(BF16),
        "g_post_mix": row(g_post_mix), "g_pre_mlp": row(g_pre_mlp), "w_up": w_up[l].astype(BF16),
        "w_down": w_down[l].astype(BF16), "g_post_mlp": row(g_post_mlp),
    }


def _layer(x, tab, lw):
    q, k, vt, *qkvs = _pre_call(x, tab, lw)
    a = _mla_call(q, k, vt)
    b = _dilated(qkvs)
    return _post_call(x, a, b, lw)


def kernel(x_prompt, x_sample, g_pre_mix, w_in, g_q_lat, w_uq, g_kv_lat, w_ukv, g_out_mla, g_out_dil,
           w_out, g_post_mix, g_pre_mlp, w_up, w_down, g_post_mlp):
    depth = w_in.shape[0]
    layers = [_layer_weights(l, g_pre_mix, w_in, g_q_lat, w_uq, g_kv_lat, w_ukv, g_out_mla,
                             g_out_dil, w_out, g_post_mix, g_pre_mlp, w_up, w_down, g_post_mlp)
              for l in range(depth)]

    def trunk(x):
        tab = _rope_table(x.shape[1])
        for lw in layers:
            x = _layer(x, tab, lw)
        return x

    return (trunk(x_prompt), trunk(x_sample))
```

```python
import functools
import math

import jax
import jax.numpy as jnp
from jax import lax
from jax.experimental import pallas as pl
from jax.experimental.pallas import tpu as pltpu

F32 = jnp.float32
BF16 = jnp.bfloat16

D_MODEL = 1024
MLA_HEADS = 8
MLA_NOPE = 64
MLA_ROPE = 32
MLA_V = 64
Q_LORA = 256
KV_LORA = 128
ROPE_THETA = 10000.0
DIL_HEADS = 8
DIL_HEAD_DIM = 64
DIL_PAIRS = ((128, 1), (512, 4), (2048, 16))
D_FF = 4 * D_MODEL
NORM_EPS = 1e-6

LANES = 128
HEAD_PAD = LANES
MLA_QK = MLA_HEADS * HEAD_PAD
MLA_OUT = MLA_HEADS * MLA_V
DIL_W = DIL_HEADS * DIL_HEAD_DIM
Z_KROPE = Q_LORA + KV_LORA
Z_DIL = Z_KROPE + LANES
Z_COLS = Z_DIL + 3 * DIL_W
LOG2E = 1.4426950408889634
MLA_QSCALE = LOG2E / math.sqrt(MLA_NOPE + MLA_ROPE)
DIL_QSCALE = LOG2E / math.sqrt(DIL_HEAD_DIM)
NEG_BIG = -1e30
MASK_DIST = 1e30
DIL_SIDE = DIL_PAIRS[0][0] // (2 * DIL_PAIRS[0][1])
DIL_STEP = DIL_PAIRS[1][1] // DIL_PAIRS[0][1]
DIL_TQ = 128
DIL_STEP_ROWS = 1024
ONES_ROWS = 16

TOKEN_TILE = 512
PRE_PARTS = 2
POST_PARTS = 2
MLA_TQ = 2048
MLA_TK = 512
MLA_GROUP = 4
MLA_HEADS_PER_STEP = 2
VMEM_LIMIT = 56 * 1024 * 1024

NT_DIMS = (((1,), (1,)), ((), ()))


def _rms(x, g):
    return x * lax.rsqrt(jnp.mean(x * x, axis=-1, keepdims=True) + NORM_EPS) * g


def _layer_spec(layer, shape, single_buffer=False):
    index_map = lambda *_: (layer,) + (0,) * len(shape)
    if single_buffer:
        return pl.BlockSpec((None,) + shape, index_map, pipeline_mode=pl.Buffered(1))
    return pl.BlockSpec((None,) + shape, index_map)


def _params(n_axes):
    return pltpu.CompilerParams(dimension_semantics=("arbitrary",) * n_axes,
                                vmem_limit_bytes=VMEM_LIMIT)


def _rope(x, cos_t, sin_lo, sin_hi):
    half = MLA_ROPE // 2
    return x * cos_t + pltpu.roll(x, LANES - half, 1) * sin_lo + pltpu.roll(x, half, 1) * sin_hi


def _pre_kernel(x_ref, tab_ref, g_pre_ref, w_in_ref, g_q_ref, w_uq_ref, g_kv_ref, w_k_ref, w_uvt_ref,
                q_ref, k_ref, vt_ref, *rest):
    dil_refs, zd_ref = rest[:-1], rest[-1]
    n_blk, ts, _ = zd_ref.shape
    parts = [slice(i * ts // PRE_PARTS, (i + 1) * ts // PRE_PARTS) for i in range(PRE_PARTS)]
    zs = [jnp.dot(_rms(x_ref[0, rows, :], g_pre_ref[...]).astype(BF16), w_in_ref[...],
                  preferred_element_type=F32) for rows in parts]
    for rows, z in zip(parts, zs):
        n = rows.stop - rows.start
        cos_t = tab_ref[rows, 0:LANES]
        sin_lo = tab_ref[rows, LANES:2 * LANES]
        sin_hi = tab_ref[rows, 2 * LANES:3 * LANES]

        lat_q = _rms(z[:, :Q_LORA], g_q_ref[...]).astype(BF16)
        q = jnp.dot(lat_q, w_uq_ref[...], preferred_element_type=F32)
        for hd in range(MLA_HEADS):
            sl = slice(hd * HEAD_PAD, (hd + 1) * HEAD_PAD)
            q_ref[0, rows, sl] = (_rope(q[:, sl], cos_t, sin_lo, sin_hi) * MLA_QSCALE).astype(BF16)

        lat_kv = _rms(z[:, Q_LORA:Q_LORA + KV_LORA], g_kv_ref[...])
        k_rope = _rope(z[:, Z_KROPE:Z_KROPE + LANES], cos_t, sin_lo, sin_hi)
        k_in = jnp.concatenate([lat_kv, k_rope], axis=1).astype(BF16)
        k_ref[0, rows, :] = jnp.dot(k_in, w_k_ref[...], preferred_element_type=F32).astype(BF16)

        vt = lax.dot_general(w_uvt_ref[...], lat_kv.astype(BF16), NT_DIMS,
                             preferred_element_type=F32)
        tkb = vt_ref.shape[3]
        assert tkb % n == 0 or n % tkb == 0
        for j in range(max(n // tkb, 1)):
            w = min(n, tkb)
            col = rows.start + j * w
            vt_ref[0, col // tkb, :, col % tkb:col % tkb + w] = vt[:, j * w:(j + 1) * w].astype(BF16)

        for j in range(n_blk):
            blk = z[:, Z_DIL + j * LANES:Z_DIL + (j + 1) * LANES]
            zd_ref[j, rows, :] = blk * DIL_QSCALE if j * LANES < DIL_W else blk
        for (_, d), ref in zip(DIL_PAIRS, dil_refs):
            out_rows = slice(rows.start // d, rows.stop // d)
            for c in range(d):
                src = pl.ds(rows.start + c, n // d, stride=d) if d > 1 else rows
                for j in range(n_blk):
                    ref[0, c, out_rows, j * LANES:(j + 1) * LANES] = zd_ref[j, src, :].astype(BF16)


def _pre_call(x, tab, lw, layer):
    B, S, D = x.shape
    ts, tk = TOKEN_TILE, MLA_TK
    tok = lambda w: pl.BlockSpec((1, ts, w), lambda b, i: (b, i, 0))
    _const_spec = functools.partial(_layer_spec, layer)
    return pl.pallas_call(
        _pre_kernel,
        grid=(B, S // ts),
        in_specs=[
            tok(D),
            pl.BlockSpec((ts, 3 * LANES), lambda b, i: (i, 0)),
            _const_spec((1, D)), _const_spec((D, Z_COLS)),
            _const_spec((1, Q_LORA)), _const_spec((Q_LORA, MLA_QK)),
            _const_spec((1, KV_LORA)), _const_spec((2 * LANES, MLA_QK)),
            _const_spec((MLA_OUT, KV_LORA)),
        ],
        out_specs=[
            tok(MLA_QK), tok(MLA_QK),
            pl.BlockSpec((1, max(ts // tk, 1), MLA_OUT, min(ts, tk)),
                         lambda b, i: (b, i * ts // tk if ts < tk else i, 0, i % max(tk // ts, 1))),
        ] + [pl.BlockSpec((1, d, ts // d, 3 * DIL_W), lambda b, i: (b, 0, i, 0)) for _, d in DIL_PAIRS],
        out_shape=[
            jax.ShapeDtypeStruct((B, S, MLA_QK), BF16),
            jax.ShapeDtypeStruct((B, S, MLA_QK), BF16),
            jax.ShapeDtypeStruct((B, S // tk, MLA_OUT, tk), BF16),
        ] + [jax.ShapeDtypeStruct((B, d, S // d, 3 * DIL_W), BF16) for _, d in DIL_PAIRS],
        scratch_shapes=[pltpu.VMEM((3 * DIL_W // LANES, ts, LANES), F32)],
        compiler_params=_params(2),
        name="pre_mix",
    )(x, tab, lw["g_pre_mix"], lw["w_in"], lw["g_q_lat"], lw["w_uq"], lw["g_kv_lat"], lw["w_k"],
      lw["w_uvt"])


def _mla_kernel(q_ref, k_ref, vt_ref, o_ref, s_ref, acc_ref, *, n_group):
    tq = q_ref.shape[1]
    nk, _, tk = vt_ref.shape[1:]
    ones = (lax.broadcasted_iota(jnp.int32, (ONES_ROWS, tk), 0) == 0).astype(BF16)
    heads = range(q_ref.shape[2] // HEAD_PAD)
    qs = [q_ref[0, :, hh * HEAD_PAD:(hh + 1) * HEAD_PAD] for hh in heads]

    def scores(t, slot):
        off = pl.multiple_of(t * tk, tk)
        tile_max = []
        for hh in heads:
            kt = k_ref[0, pl.ds(off, tk), hh * HEAD_PAD:(hh + 1) * HEAD_PAD]
            s = lax.dot_general(kt, qs[hh], NT_DIMS, preferred_element_type=F32)
            s_ref[slot, hh] = s
            tile_max.append(jnp.max(s, axis=0, keepdims=True))
        return tuple(tile_max)

    def attend(t, slot, ms, tile_max):
        new = []
        for hh in heads:
            m_new = jnp.maximum(ms[hh], tile_max[hh])
            alpha = jnp.exp2(ms[hh] - m_new)
            p = jnp.exp2(s_ref[slot, hh] - m_new).astype(BF16)
            vt = vt_ref[0, t, hh * MLA_V:(hh + 1) * MLA_V, :]
            pv = jnp.dot(jnp.concatenate([vt, ones], axis=0), p, preferred_element_type=F32)
            acc_ref[hh] = alpha * acc_ref[hh] + pv
            new.append(m_new)
        return tuple(new)

    def group(u, carry, last_group=False):
        ms, tile_max = carry
        for j in range(n_group):
            t = n_group * u + j
            next_max = None
            if j + 1 < n_group or not last_group:
                next_max = scores(t + 1, (j + 1) % 2)
            ms = attend(t, j % 2, ms, tile_max)
            tile_max = next_max
        return ms, tile_max

    acc_ref[...] = jnp.zeros_like(acc_ref)
    carry = (jnp.full((1, tq), NEG_BIG, F32),) * len(heads), scores(0, 0)
    if nk > n_group:
        carry = lax.fori_loop(0, nk // n_group - 1, group, carry)
    group(nk // n_group - 1, carry, last_group=True)
    outs = [acc_ref[hh, :MLA_V] / acc_ref[hh, MLA_V:MLA_V + 1] for hh in heads]
    o_ref[0] = jnp.concatenate(outs, axis=0).T


def _mla_call(q, k, vt):
    B, S, _ = q.shape
    nk, _, tk = vt.shape[1:]
    n_group = min(MLA_GROUP, nk)
    tq = min(MLA_TQ, S)
    assert nk % n_group == 0 and n_group % 2 == 0 and S % tq == 0
    nh = MLA_HEADS_PER_STEP
    return pl.pallas_call(
        functools.partial(_mla_kernel, n_group=n_group),
        grid=(B, MLA_HEADS // nh, S // tq),
        in_specs=[
            pl.BlockSpec((1, tq, nh * HEAD_PAD), lambda b, hp, i: (b, i, hp)),
            pl.BlockSpec((1, S, nh * HEAD_PAD), lambda b, hp, i: (b, 0, hp)),
            pl.BlockSpec((1, nk, nh * MLA_V, tk), lambda b, hp, i: (b, 0, hp, 0)),
        ],
        out_specs=pl.BlockSpec((1, tq, nh * MLA_V), lambda b, hp, i: (b, i, hp)),
        out_shape=jax.ShapeDtypeStruct((B, S, MLA_OUT), F32),
        scratch_shapes=[pltpu.VMEM((2, nh, tk, tq), F32),
                        pltpu.VMEM((nh, MLA_V + ONES_ROWS, tq), F32)],
        compiler_params=_params(3),
        name="mla_attn",
    )(q, k, vt)


def _dil_kernel(*refs, d, tq, win, first, last):
    q_ref, k_ref, v_ref = refs[:3]
    refs = refs[3:]
    if not first:
        acc_in, st_in = refs[:2]
        refs = refs[2:]
    n_out = 1 if last else 2
    outs, (s_ref, bias_ref, acc_scr, st_scr) = refs[:n_out], refs[n_out:]

    r = pl.program_id(2)
    seq = k_ref.shape[2]
    n_cls, rows_c = q_ref.shape[1:3]
    n_sub = rows_c // tq
    heads = range(DIL_HEADS)
    per_blk = LANES // DIL_HEAD_DIM
    lane_head = lax.broadcasted_iota(jnp.int32, (tq, LANES), 1) // DIL_HEAD_DIM
    ones = (lax.broadcasted_iota(jnp.int32, (ONES_ROWS, win), 0) == 0).astype(BF16)

    @pl.when((pl.program_id(0) == 0) & (pl.program_id(1) == 0) & (r == 0))
    def _fill_bias_tables():
        kk = lax.broadcasted_iota(jnp.int32, (win, tq), 0)
        qq = lax.broadcasted_iota(jnp.int32, (win, tq), 1)
        for shift in range(bias_ref.shape[0]):
            rel = jnp.abs(kk - qq - shift * DIL_SIDE)
            dist = jnp.where(rel <= DIL_SIDE, rel.astype(F32), MASK_DIST)
            for hd in heads:
                slope = 2.0 ** (-8.0 * (hd + 1.0) / DIL_HEADS)
                bias_ref[shift, hd] = -(slope * d * LOG2E) * dist

    def window_start(sub):
        t0 = (r * n_sub + sub) * tq
        return t0, pl.multiple_of(jnp.clip(t0 - DIL_SIDE, 0, seq - win), DIL_SIDE)

    def scores(cl, sub):
        t0, start = window_start(sub)
        shift = (t0 - start) // DIL_SIDE
        tile_max = []
        for hd in heads:
            cs = slice(hd // per_blk * LANES, (hd // per_blk + 1) * LANES)
            q = q_ref[0, cl, sub * tq:(sub + 1) * tq, cs]
            q_hd = jnp.where(lane_head == hd % per_blk, q, jnp.zeros_like(q))
            s = lax.dot_general(k_ref[0, cl, pl.ds(start, win), cs], q_hd, NT_DIMS,
                                preferred_element_type=F32) + bias_ref[shift, hd]
            s_ref[cl * n_sub + sub, hd] = s
            tile_max.append(jnp.max(s, axis=0, keepdims=True))
        return tile_max

    def attend(cl, sub, tile_max):
        _, start = window_start(sub)
        qrows = slice(sub * tq, (sub + 1) * tq)
        srows = slice(cl * rows_c + sub * tq, cl * rows_c + (sub + 1) * tq)
        vt = v_ref[0, cl, pl.ds(start, win), :].T
        if not first:
            acc_prev = acc_in[0, cl, qrows, :].T
            st_prev = st_in[0, cl, qrows, :].T
        accs, ms, ls = [], [], []
        for hd in heads:
            rows = slice(hd * DIL_HEAD_DIM, (hd + 1) * DIL_HEAD_DIM)
            m_new = tile_max[hd]
            if not first:
                m_prev = st_prev[hd:hd + 1]
                m_new = jnp.maximum(m_prev, m_new)
            p = jnp.exp2(s_ref[cl * n_sub + sub, hd] - m_new).astype(BF16)
            pv = jnp.dot(jnp.concatenate([vt[rows], ones], axis=0), p, preferred_element_type=F32)
            acc, l_new = pv[:DIL_HEAD_DIM], pv[DIL_HEAD_DIM:DIL_HEAD_DIM + 1]
            if not first:
                alpha = jnp.exp2(m_prev - m_new)
                l_new = alpha * st_prev[DIL_HEADS + hd:DIL_HEADS + hd + 1] + l_new
                acc = alpha * acc_prev[rows] + acc
            if last:
                acc = acc / l_new
            accs.append(acc)
            ms.append(m_new)
            ls.append(l_new)
        out_tile = jnp.concatenate(accs, axis=0).T
        if last:
            outs[0][0, cl, qrows, :] = out_tile
        else:
            pad = jnp.zeros((LANES - 2 * DIL_HEADS, tq), F32)
            st_scr[0, srows, :] = jnp.concatenate(ms + ls + [pad], axis=0).T
            for j in range(acc_scr.shape[0]):
                acc_scr[j, srows, :] = out_tile[:, j * LANES:(j + 1) * LANES]

    units = [(cl, sub) for cl in range(n_cls) for sub in range(n_sub)]
    tile_max = scores(*units[0])
    for i, unit in enumerate(units):
        next_max = scores(*units[i + 1]) if i + 1 < len(units) else None
        attend(*unit, tile_max)
        tile_max = next_max
    if not last:
        for scr, out in zip((acc_scr, st_scr), outs):
            for cl in range(n_cls):
                for k in range(DIL_STEP):
                    rows = pl.ds(cl * rows_c + k, rows_c // DIL_STEP, stride=DIL_STEP)
                    for j in range(scr.shape[0]):
                        out[0, k, cl, :, j * LANES:(j + 1) * LANES] = scr[j, rows, :]


def _dil_call(d, qkv, state, first, last):
    B, _, seq, _ = qkv.shape
    win = min(DIL_TQ + 2 * DIL_SIDE, seq)
    tq = min(DIL_STEP_ROWS, seq)
    n_cls = max(1, min(d, DIL_STEP_ROWS // seq))
    assert seq % tq == 0 and tq % DIL_TQ == 0 and d % n_cls == 0
    tile = lambda w: pl.BlockSpec((1, n_cls, tq, w), lambda b, c, r: (b, c, r, 0))
    kv = lambda j: pl.BlockSpec((1, n_cls, seq, DIL_W), lambda b, c, r: (b, c, 0, j))
    in_specs = [tile(DIL_W), kv(1), kv(2)]
    args = [qkv, qkv, qkv]
    if not first:
        in_specs += [tile(DIL_W), tile(LANES)]
        args += list(state)
    if last:
        out_specs = tile(DIL_W)
        out_shape = jax.ShapeDtypeStruct((B, d, seq, DIL_W), F32)
    else:
        refold = lambda w: pl.BlockSpec((1, DIL_STEP, n_cls, tq // DIL_STEP, w),
                                        lambda b, c, r: (b, 0, c, r, 0))
        widths = (DIL_W, LANES)
        out_specs = [refold(w) for w in widths]
        out_shape = [jax.ShapeDtypeStruct((B, DIL_STEP, d, seq // DIL_STEP, w), F32) for w in widths]
    out = pl.pallas_call(
        functools.partial(_dil_kernel, d=d, tq=DIL_TQ, win=win, first=first, last=last),
        grid=(B, d // n_cls, seq // tq),
        in_specs=in_specs,
        out_specs=out_specs,
        out_shape=out_shape,
        scratch_shapes=[pltpu.VMEM((n_cls * tq // DIL_TQ, DIL_HEADS, win, DIL_TQ), F32),
                        pltpu.VMEM(((win - DIL_TQ) // DIL_SIDE + 1, DIL_HEADS, win, DIL_TQ), F32),
                        pltpu.VMEM((DIL_W // LANES, n_cls * tq, LANES), F32),
                        pltpu.VMEM((1, n_cls * tq, LANES), F32)],
        compiler_params=_params(3),
        name=f"dilated_d{d}",
    )(*args)
    if last:
        return out
    return [t.reshape(B, DIL_STEP * d, seq // DIL_STEP, t.shape[-1]) for t in out]


def _dilated(qkvs):
    state = None
    n = len(DIL_PAIRS)
    for j, ((window, d), qkv) in enumerate(zip(DIL_PAIRS, qkvs)):
        assert window // (2 * d) == DIL_SIDE and (j == 0 or d == DIL_STEP * DIL_PAIRS[j - 1][1])
        state = _dil_call(d, qkv, state, first=(j == 0), last=(j == n - 1))
    return state


def _post_kernel(x_ref, a_ref, b_ref, g_a_ref, g_b_ref, w_out_ref, g_mix_ref, g_pre_ref, w_up_ref,
                 w_down_ref, g_post_ref, o_ref, b_scr):
    d_last = b_ref.shape[1]
    for c in range(d_last):
        rows = pl.ds(c, b_ref.shape[2], stride=d_last)
        for j in range(b_scr.shape[0]):
            b_scr[j, rows, :] = b_ref[0, c, :, j * LANES:(j + 1) * LANES]
    ts = x_ref.shape[1]
    parts = [slice(i * ts // POST_PARTS, (i + 1) * ts // POST_PARTS) for i in range(POST_PARTS)]
    mixes = []
    for rows in parts:
        b = jnp.concatenate([b_scr[j, rows, :] for j in range(b_scr.shape[0])], axis=1)
        an = _rms(a_ref[0, rows, :], g_a_ref[...]).astype(BF16)
        bn = _rms(b, g_b_ref[...]).astype(BF16)
        mix = jnp.dot(an, w_out_ref[:MLA_OUT, :], preferred_element_type=F32)
        mixes.append(mix + jnp.dot(bn, w_out_ref[MLA_OUT:, :], preferred_element_type=F32))
    for rows, mix in zip(parts, mixes):
        x1 = x_ref[0, rows, :] + _rms(mix, g_mix_ref[...])
        h = _rms(x1, g_pre_ref[...]).astype(BF16)
        y = jnp.zeros_like(x1)
        for c in range(D_FF // D_MODEL):
            cs = slice(c * D_MODEL, (c + 1) * D_MODEL)
            u = jnp.dot(h, w_up_ref[:, cs], preferred_element_type=F32)
            u = jnp.square(jnp.maximum(u, 0.0)).astype(BF16)
            y = y + jnp.dot(u, w_down_ref[cs, :], preferred_element_type=F32)
        o_ref[0, rows, :] = x1 + _rms(y, g_post_ref[...])


def _post_call(x, a, b, lw, layer):
    B, S, D = x.shape
    ts = TOKEN_TILE
    d_last = b.shape[1]
    tok = lambda w: pl.BlockSpec((1, ts, w), lambda bi, i: (bi, i, 0))
    _const_spec = functools.partial(_layer_spec, layer)
    return pl.pallas_call(
        _post_kernel,
        grid=(B, S // ts),
        in_specs=[
            tok(D), tok(MLA_OUT),
            pl.BlockSpec((1, d_last, ts // d_last, DIL_W), lambda bi, i: (bi, 0, i, 0)),
            _const_spec((1, MLA_OUT)), _const_spec((1, DIL_W)),
            _const_spec((D, D), True), _const_spec((1, D)), _const_spec((1, D)),
            _const_spec((D, D_FF), True), _const_spec((D_FF, D), True), _const_spec((1, D)),
        ],
        out_specs=tok(D),
        out_shape=jax.ShapeDtypeStruct((B, S, D), F32),
        scratch_shapes=[pltpu.VMEM((DIL_W // LANES, ts, LANES), F32)],
        compiler_params=_params(2),
        name="post_mix_mlp",
    )(x, a, b, lw["g_out_mla"], lw["g_out_dil"], lw["w_out"], lw["g_post_mix"], lw["g_pre_mlp"],
      lw["w_up"], lw["w_down"], lw["g_post_mlp"])


def _rope_table(S):
    half = MLA_ROPE // 2
    inv_freq = ROPE_THETA ** (-jnp.arange(half, dtype=F32) / half)
    ang = jnp.arange(S, dtype=F32)[:, None] * inv_freq[None, :]
    cos, sin = jnp.cos(ang), jnp.sin(ang)
    ones = jnp.ones((S, MLA_NOPE), F32)
    zh = jnp.zeros((S, half), F32)
    z32 = jnp.zeros((S, LANES - MLA_NOPE - MLA_ROPE), F32)
    z64 = jnp.zeros((S, MLA_NOPE), F32)
    cos_t = jnp.concatenate([ones, cos, cos, z32], axis=1)
    sin_lo = jnp.concatenate([z64, -sin, zh, z32], axis=1)
    sin_hi = jnp.concatenate([z64, zh, sin, z32], axis=1)
    return jnp.concatenate([cos_t, sin_lo, sin_hi], axis=1)


def _prep_weights(g_pre_mix, w_in, g_q_lat, w_uq, g_kv_lat, w_ukv, g_out_mla, g_out_dil, w_out,
                  g_post_mix, g_pre_mlp, w_up, w_down, g_post_mlp):
    depth = w_in.shape[0]
    row = lambda g: g[:, None, :].astype(F32)
    wi = w_in.astype(BF16)
    c1 = Q_LORA + KV_LORA
    c2 = c1 + MLA_ROPE
    pad = lambda n: jnp.zeros((depth, D_MODEL, n), BF16)
    w_in_p = jnp.concatenate(
        [wi[..., :c1], pad(MLA_NOPE), wi[..., c1:c2], pad(LANES - MLA_NOPE - MLA_ROPE), wi[..., c2:]],
        axis=2)
    head_pad = lambda w, n: jnp.pad(w.astype(BF16), ((0, 0), (0, 0), (0, 0), (0, HEAD_PAD - n)))
    w_uq_p = head_pad(w_uq, MLA_NOPE + MLA_ROPE)
    w_uk_p = head_pad(w_ukv[..., :MLA_NOPE], MLA_NOPE)
    place = jnp.zeros((LANES, MLA_HEADS, HEAD_PAD), BF16)
    idx = MLA_NOPE + jnp.arange(MLA_ROPE)
    place = jnp.broadcast_to(place.at[idx, :, idx].set(1.0), (depth,) + place.shape)
    w_k = jnp.concatenate([w_uk_p, place], axis=1).reshape(depth, 2 * LANES, MLA_QK)
    w_uv = w_ukv[..., MLA_NOPE:].reshape(depth, KV_LORA, MLA_OUT).astype(BF16)
    return {
        "g_pre_mix": row(g_pre_mix), "w_in": w_in_p,
        "g_q_lat": row(g_q_lat), "w_uq": w_uq_p.reshape(depth, Q_LORA, MLA_QK),
        "g_kv_lat": row(g_kv_lat), "w_k": w_k, "w_uvt": jnp.swapaxes(w_uv, 1, 2),
        "g_out_mla": row(g_out_mla), "g_out_dil": row(g_out_dil), "w_out": w_out.astype(BF16),
        "g_post_mix": row(g_post_mix), "g_pre_mlp": row(g_pre_mlp), "w_up": w_up.astype(BF16),
        "w_down": w_down.astype(BF16), "g_post_mlp": row(g_post_mlp),
    }


def _layer(x, tab, lw, layer):
    q, k, vt, *qkvs = _pre_call(x, tab, lw, layer)
    a = _mla_call(q, k, vt)
    b = _dilated(qkvs)
    return _post_call(x, a, b, lw, layer)


def kernel(x_prompt, x_sample, g_pre_mix, w_in, g_q_lat, w_uq, g_kv_lat, w_ukv, g_out_mla, g_out_dil,
           w_out, g_post_mix, g_pre_mlp, w_up, w_down, g_post_mlp):
    lw = _prep_weights(g_pre_mix, w_in, g_q_lat, w_uq, g_kv_lat, w_ukv, g_out_mla, g_out_dil, w_out,
                       g_post_mix, g_pre_mlp, w_up, w_down, g_post_mlp)
    tab = _rope_table(max(x_prompt.shape[1], x_sample.shape[1]))

    def trunk(x):
        for layer in range(w_in.shape[0]):
            x = _layer(x, tab, lw, layer)
        return x

    return (trunk(x_prompt), trunk(x_sample))
```
